```python
import math
import jax, jax.numpy as jnp
from jax import lax
import numpy as np


D_MODEL = 1024
BATCH = 2
SEQ = 8192
DEPTH = 2
DEC_BATCH = 32
DEC_SEQ = 4
PAST_LEN = 16384
PAGE_SIZE = 128

D_MIX = D_MODEL
D_CONV = D_MIX // 4
CONV_GROUPS = 4
CONV_WIDTH = 3
D_HGRN = D_MIX // 4
HGRN_HEADS = 4
HGRN_DK = D_HGRN // HGRN_HEADS
HGRN_DV = D_HGRN // HGRN_HEADS
HGRN_CHUNK = 64
D_SB = D_MIX // 2
SB_HEADS = 8
SB_HEAD_DIM = D_SB // SB_HEADS
SB_BLOCK = 128
SB_BIAS_INIT = -6.0
D_FF = 2816
RMS_EPS = 1e-6
IN_SIZES = [D_CONV] * 3 + [D_HGRN] * 4 + [D_SB] * 3
D_IN = sum(IN_SIZES)

kernel_name = 'hymba_conv_hgrn2_stickbreaking_macaron_step'


def rms_norm(x, g):
    xf = x.astype(jnp.float32)
    y = xf * lax.rsqrt(jnp.mean(xf * xf, axis=-1, keepdims=True) + RMS_EPS)
    return (y * g.astype(jnp.float32)).astype(x.dtype)


def swiglu(x, wg, wu, wd):
    return (jax.nn.silu(x @ wg) * (x @ wu)) @ wd


def split_in(h):
    idx = np.cumsum(IN_SIZES)[:-1].tolist()
    return jnp.split(h, idx, axis=-1)


def short_conv_mixer(bg, cg, xa, w, prev):
    u = cg * xa
    up = jnp.concatenate([prev.astype(u.dtype), u], axis=1)
    T = u.shape[1]
    y = w[0] * up[:, 0:T]
    for j in range(1, CONV_WIDTH):
        y = y + w[j] * up[:, j:j + T]
    return bg * y, up[:, -(CONV_WIDTH - 1):]


def hgrn_lower_bounds(lb_logits):
    p = jax.nn.softmax(lb_logits.astype(jnp.float32), axis=0)
    c = jnp.cumsum(p, axis=0)
    return c - c[0:1]


def hgrn2_recurrence(q, logf, k, v, s0):
    B, T, H, DK = q.shape
    C = math.gcd(T, HGRN_CHUNK)
    nc = T // C

    def to_chunks(a):
        return a.reshape(B, nc, C, *a.shape[2:]).swapaxes(0, 1)

    causal = jnp.tril(jnp.ones((C, C), dtype=bool))

    def step(S, inp):
        qc, lfc, kc, vc = inp
        b = jnp.cumsum(lfc, axis=1)
        o_inter = jnp.einsum('bthk,bhkv->bthv', qc * jnp.exp(b), S)
        diff = b[:, :, None] - b[:, None, :]
        decay = jnp.exp(jnp.where(causal[None, :, :, None, None], diff, -jnp.inf))
        att = jnp.einsum('bthk,bshk,btshk->bhts', qc, kc, decay)
        o_intra = jnp.einsum('bhts,bshv->bthv', att, vc)
        b_last = b[:, -1]
        k_dec = kc * jnp.exp(b_last[:, None] - b)
        S_new = jnp.exp(b_last)[..., None] * S + jnp.einsum('bshk,bshv->bhkv', k_dec, vc)
        return S_new, o_inter + o_intra

    S_fin, o = lax.scan(step, s0, (to_chunks(q), to_chunks(logf), to_chunks(k), to_chunks(v)))
    o = o.swapaxes(0, 1).reshape(B, T, H, v.shape[-1])
    return o, S_fin


def hgrn2_mixer(hq, hf, hi, hg, lb, gnorm, s0):
    B, T, _ = hq.shape
    hs = (B, T, HGRN_HEADS, HGRN_DK)
    lbh = lb.reshape(HGRN_HEADS, HGRN_DK)
    z = hf.astype(jnp.float32).reshape(hs)
    logf = jnp.logaddexp(jnp.log(lbh), jnp.log1p(-lbh) + jax.nn.log_sigmoid(z))
    k_in = (1.0 - lbh) * jax.nn.sigmoid(-z)
    q = hq.astype(jnp.float32).reshape(hs)
    v = hi.astype(jnp.float32).reshape(B, T, HGRN_HEADS, HGRN_DV)
    o, s_new = hgrn2_recurrence(q, logf, k_in, v, s0.astype(jnp.float32))
    o = o * lax.rsqrt(jnp.mean(o * o, axis=-1, keepdims=True) + RMS_EPS)
    o = o.reshape(B, T, D_HGRN) * gnorm.astype(jnp.float32)
    out = o * jax.nn.silu(hg.astype(jnp.float32))
    return out.astype(hq.dtype), s_new.astype(hq.dtype)


def stick_breaking_block(qb, k, v, q_pos, k_pos, bias):
    z = jnp.einsum('bqhd,bkhd->bhqk', qb, k).astype(jnp.float32) * (SB_HEAD_DIM ** -0.5)
    z = z + bias.astype(jnp.float32)[None, :, None, None]
    mask = k_pos[None, :] < q_pos[:, None]
    log_not = jnp.where(mask, jax.nn.log_sigmoid(-z), 0.0)
    after = lax.cumsum(log_not, axis=3, reverse=True) - log_not
    w = jnp.where(mask, jnp.exp(jax.nn.log_sigmoid(z) + after), 0.0)
    return jnp.einsum('bhqk,bkhd->bqhd', w.astype(v.dtype), v)


def stick_breaking_prompt(q, k, v, bias):
    B, T, H, dh = q.shape
    nb = T // SB_BLOCK
    k_pos = jnp.arange(T)
    qb = q.reshape(B, nb, SB_BLOCK, H, dh).swapaxes(0, 1)

    def blk(args):
        qi, i = args
        q_pos = i * SB_BLOCK + jnp.arange(SB_BLOCK)
        return stick_breaking_block(qi, k, v, q_pos, k_pos, bias)

    o = lax.map(blk, (qb, jnp.arange(nb)))
    return o.swapaxes(0, 1).reshape(B, T, H * dh)


def trunk_layer(x, conv_prev, s0, k_past, v_past, lb, norm1, w1g, w1u, w1d, norm_mx,
                w_in, conv_w, gnorm, sb_bias, w_out, norm2, w2g, w2u, w2d):
    B, T, _ = x.shape
    x = x + 0.5 * swiglu(rms_norm(x, norm1), w1g, w1u, w1d)
    h = rms_norm(x, norm_mx) @ w_in
    bg, cg, xa, hq, hf, hi, hg, sq, sk, sv = split_in(h)
    a_out, conv_new = short_conv_mixer(bg, cg, xa, conv_w, conv_prev)
    b_out, s_new = hgrn2_mixer(hq, hf, hi, hg, lb, gnorm, s0)
    q = sq.reshape(B, T, SB_HEADS, SB_HEAD_DIM)
    k = sk.reshape(B, T, SB_HEADS, SB_HEAD_DIM)
    v = sv.reshape(B, T, SB_HEADS, SB_HEAD_DIM)
    if k_past is None:
        c_out = stick_breaking_prompt(q, k, v, sb_bias)
    else:
        Lp = k_past.shape[1]
        kk = jnp.concatenate([k_past.astype(k.dtype), k], axis=1)
        vv = jnp.concatenate([v_past.astype(v.dtype), v], axis=1)
        c_out = stick_breaking_block(q, kk, vv, Lp + jnp.arange(T), jnp.arange(Lp + T),
                                     sb_bias).reshape(B, T, D_SB)
    x = x + jnp.concatenate([a_out, b_out, c_out], axis=-1) @ w_out
    x = x + 0.5 * swiglu(rms_norm(x, norm2), w2g, w2u, w2d)
    return x, conv_new, s_new, k, v


def setup_inputs(seed: int = 0) -> dict:
    key = jax.random.key(seed)
    ks = jax.random.split(key, 24)
    n_pages = PAST_LEN // PAGE_SIZE
    n_pool = (DEC_BATCH * n_pages * 5) // 4
    f32 = jnp.float32

    def nrm(k, shape, s):
        return jax.random.normal(k, shape, f32) * s

    page_table = jax.random.permutation(ks[6], n_pool)[:DEC_BATCH * n_pages]
    page_table = page_table.reshape(DEC_BATCH, n_pages).astype(jnp.int32)
    return {
        'x_prompt': nrm(ks[0], (BATCH, SEQ, D_MODEL), 1.0),
        'x_sample': nrm(ks[1], (DEC_BATCH, DEC_SEQ, D_MODEL), 1.0),
        'state_conv': nrm(ks[2], (DEPTH, DEC_BATCH, CONV_WIDTH - 1, D_CONV), 1.0),
        'state_hgrn': nrm(ks[3], (DEPTH, DEC_BATCH, HGRN_HEADS, HGRN_DK, HGRN_DV), 0.3),
        'cache_k': nrm(ks[4], (DEPTH, n_pool, PAGE_SIZE, SB_HEADS, SB_HEAD_DIM), 0.5),
        'cache_v': nrm(ks[5], (DEPTH, n_pool, PAGE_SIZE, SB_HEADS, SB_HEAD_DIM), 1.0),
        'page_table': page_table,
        'norm_ffn1': 1.0 + nrm(ks[7], (DEPTH, D_MODEL), 0.01),
        'w_ffn1_gate': nrm(ks[8], (DEPTH, D_MODEL, D_FF), D_MODEL ** -0.5),
        'w_ffn1_up': nrm(ks[9], (DEPTH, D_MODEL, D_FF), D_MODEL ** -0.5),
        'w_ffn1_down': nrm(ks[10], (DEPTH, D_FF, D_MODEL), D_FF ** -0.5),
        'norm_mix': 1.0 + nrm(ks[11], (DEPTH, D_MODEL), 0.01),
        'w_in': nrm(ks[12], (DEPTH, D_MODEL, D_IN), D_MODEL ** -0.5),
        'conv_w': nrm(ks[13], (DEPTH, CONV_WIDTH, D_CONV), 0.5),
        'hgrn_lb_logits': nrm(ks[14], (DEPTH, D_HGRN), 0.5),
        'hgrn_gnorm': 1.0 + nrm(ks[15], (DEPTH, D_HGRN), 0.01),
        'sb_bias': SB_BIAS_INIT + nrm(ks[22], (DEPTH, SB_HEADS), 0.1),
        'w_out': nrm(ks[16], (DEPTH, D_MIX, D_MODEL), D_MIX ** -0.5),
        'norm_ffn2': 1.0 + nrm(ks[17], (DEPTH, D_MODEL), 0.01),
        'w_ffn2_gate': nrm(ks[18], (DEPTH, D_MODEL, D_FF), D_MODEL ** -0.5),
        'w_ffn2_up': nrm(ks[19], (DEPTH, D_MODEL, D_FF), D_MODEL ** -0.5),
        'w_ffn2_down': nrm(ks[20], (DEPTH, D_FF, D_MODEL), D_FF ** -0.5),
        'norm_final': 1.0 + nrm(ks[21], (D_MODEL,), 0.01),
    }


def reference(x_prompt, x_sample, state_conv, state_hgrn, cache_k, cache_v, page_table,
              norm_ffn1, w_ffn1_gate, w_ffn1_up, w_ffn1_down, norm_mix, w_in, conv_w,
              hgrn_lb_logits, hgrn_gnorm, sb_bias, w_out, norm_ffn2, w_ffn2_gate, w_ffn2_up,
              w_ffn2_down, norm_final):
    lower_bounds = hgrn_lower_bounds(hgrn_lb_logits)
    xp, xs = x_prompt, x_sample
    Bp, Bs = xp.shape[0], xs.shape[0]
    conv_p, conv_s, hg_p, hg_s, kp_l, vp_l, ks_l, vs_l = [], [], [], [], [], [], [], []
    for l in range(DEPTH):
        lw = (lower_bounds[l], norm_ffn1[l], w_ffn1_gate[l], w_ffn1_up[l], w_ffn1_down[l],
              norm_mix[l], w_in[l], conv_w[l], hgrn_gnorm[l], sb_bias[l], w_out[l], norm_ffn2[l],
              w_ffn2_gate[l], w_ffn2_up[l], w_ffn2_down[l])
        conv0 = jnp.zeros((Bp, CONV_WIDTH - 1, D_CONV), xp.dtype)
        s00 = jnp.zeros((Bp, HGRN_HEADS, HGRN_DK, HGRN_DV), jnp.float32)
        xp, c, s, k, v = trunk_layer(xp, conv0, s00, None, None, *lw)
        conv_p.append(c); hg_p.append(s); kp_l.append(k); vp_l.append(v)
        k_past = cache_k[l][page_table].reshape(Bs, -1, SB_HEADS, SB_HEAD_DIM)
        v_past = cache_v[l][page_table].reshape(Bs, -1, SB_HEADS, SB_HEAD_DIM)
        xs, c, s, k, v = trunk_layer(xs, state_conv[l], state_hgrn[l], k_past, v_past, *lw)
        conv_s.append(c); hg_s.append(s); ks_l.append(k); vs_l.append(v)
    y_prompt = rms_norm(xp, norm_final)
    y_sample = rms_norm(xs, norm_final)
    return (y_prompt, y_sample, jnp.stack(conv_p), jnp.stack(conv_s), jnp.stack(hg_p),
            jnp.stack(hg_s), jnp.stack(kp_l), jnp.stack(vp_l), jnp.stack(ks_l), jnp.stack(vs_l))
```

```python
import functools

import jax
import jax.numpy as jnp
from jax import lax
from jax.experimental import pallas as pl
from jax.experimental.pallas import tpu as pltpu

F32 = jnp.float32
BF16 = jnp.bfloat16

RMS_EPS = 1e-6
D_CONV = 256
D_HGRN = 256
HGRN_HEAD = 64
D_SB = 512
SB_HEADS = 8
SB_HEAD_DIM = 64
SB_SCALE = SB_HEAD_DIM ** -0.5

V7X_VMEM_BYTES = 64 * 1024 * 1024
VMEM_LIMIT = V7X_VMEM_BYTES - 8 * 1024 * 1024
LANES = 128
SUBLANES = 8

FFN_ROWS = 256
HGRN_CHUNK = 64
ATT_Q = 256
ATT_K = 256
DEC_PAGES = 8


def _const_spec(shape):
    nd = len(shape)
    return pl.BlockSpec(shape, lambda *_: (0,) * nd, pipeline_mode=pl.Buffered(1))


def _smem_spec():
    return pl.BlockSpec(memory_space=pltpu.SMEM)


def _params(*sem):
    return pltpu.CompilerParams(dimension_semantics=sem, vmem_limit_bytes=VMEM_LIMIT)


def _rms(x, g):
    return x * lax.rsqrt(jnp.mean(x * x, axis=-1, keepdims=True) + RMS_EPS) * g


def _split2(x):
    hi = x.astype(BF16)
    lo = (x - hi.astype(F32)).astype(BF16)
    return hi, lo


def _split3(x):
    hi = x.astype(BF16)
    r = x - hi.astype(F32)
    mid = r.astype(BF16)
    lo = (r - mid.astype(F32)).astype(BF16)
    return hi, mid, lo


def _dot(a, b):
    return jnp.dot(a, b, preferred_element_type=F32)


def _dot_nt(a, b):
    return lax.dot_general(a, b, (((1,), (1,)), ((), ())), preferred_element_type=F32)


def _dot_tn(a, b):
    return lax.dot_general(a, b, (((0,), (0,)), ((), ())), preferred_element_type=F32)


def _log_not_sigmoid(z):
    nz = -z
    l1p = jnp.log1p(jnp.exp(jnp.minimum(z, nz)))
    log_not = jnp.minimum(nz, 0.0) - l1p
    return log_not, z + log_not


def _swiglu_residual(x, g, wg_ref, wu_ref, wd_ref):
    n = _rms(x, g).astype(BF16)
    gate = _dot(n, wg_ref[...])
    up = _dot(n, wu_ref[...])
    act = (gate * jax.nn.sigmoid(gate) * up).astype(BF16)
    return x + 0.5 * _dot(act, wd_ref[...])


def _ffn_kernel(x_ref, g_ref, wg_ref, wu_ref, wd_ref, o_ref):
    o_ref[...] = _swiglu_residual(x_ref[...], g_ref[...], wg_ref, wu_ref, wd_ref)


def _ffn(x, g, wg, wu, wd):
    n, d = x.shape
    rows = min(FFN_ROWS, n)
    tile = pl.BlockSpec((rows, d), lambda i: (i, 0))
    return pl.pallas_call(
        _ffn_kernel,
        out_shape=jax.ShapeDtypeStruct((n, d), F32),
        grid=(n // rows,),
        in_specs=[tile, _const_spec((1, d)), _const_spec(wg.shape), _const_spec(wu.shape),
                  _const_spec(wd.shape)],
        out_specs=tile,
        compiler_params=_params("parallel"),
        name="ffn",
    )(x, g, wg, wu, wd)


def _out_ffn_kernel(x_ref, a_ref, b_ref, c_ref, wo_ref, g_ref, wg_ref, wu_ref, wd_ref, gf_ref,
                    o_ref, *, final):
    mix = jnp.concatenate([a_ref[...], b_ref[...], c_ref[...]], axis=-1)
    x = x_ref[...] + _dot(mix, wo_ref[...])
    x = _swiglu_residual(x, g_ref[...], wg_ref, wu_ref, wd_ref)
    o_ref[...] = _rms(x, gf_ref[...]) if final else x


def _out_ffn(x, a, b, c, wo, g, wg, wu, wd, gf, final):
    n, d = x.shape
    rows = min(FFN_ROWS, n)

    def tile(w):
        return pl.BlockSpec((rows, w), lambda i: (i, 0))

    return pl.pallas_call(
        functools.partial(_out_ffn_kernel, final=final),
        out_shape=jax.ShapeDtypeStruct((n, d), F32),
        grid=(n // rows,),
        in_specs=[tile(d), tile(a.shape[1]), tile(b.shape[1]), tile(c.shape[1]),
                  _const_spec(wo.shape), _const_spec((1, d)), _const_spec(wg.shape),
                  _const_spec(wu.shape), _const_spec(wd.shape), _const_spec((1, d))],
        out_specs=tile(d),
        compiler_params=_params("parallel"),
        name="out_ffn",
    )(x, a, b, c, wo, g, wg, wu, wd, gf)


def _norm_matmul_kernel(x_ref, g_ref, w_ref, o_ref):
    o_ref[...] = _dot(_rms(x_ref[...], g_ref[...]).astype(BF16), w_ref[...])


def _norm_matmul(x, g, w):
    n, d = x.shape
    return pl.pallas_call(
        _norm_matmul_kernel,
        out_shape=jax.ShapeDtypeStruct((n, w.shape[1]), F32),
        grid=(1,),
        in_specs=[_const_spec((n, d)), _const_spec((1, d)), _const_spec(w.shape)],
        out_specs=pl.BlockSpec((n, w.shape[1]), lambda i: (0, 0)),
        compiler_params=_params("arbitrary"),
        name="norm_matmul",
    )(x, g, w)


def _proj_kernel(x_ref, g_ref, wc_ref, wh_ref, wq_ref, wkt_ref, wvt_ref, cw_ref,
                 a_ref, hh_ref, q_ref, kt_ref, vt_ref, ktb_ref, vtb_ref, cs_ref, carry_ref):
    @pl.when(pl.program_id(1) == 0)
    def _():
        carry_ref[...] = jnp.zeros_like(carry_ref)

    n = _rms(x_ref[...], g_ref[...]).astype(BF16)
    rows = n.shape[0]

    hc = _dot(n, wc_ref[...])
    bg, cg, xa = hc[:, :D_CONV], hc[:, D_CONV:2 * D_CONV], hc[:, 2 * D_CONV:]
    u = cg * xa
    prev = carry_ref[...]
    p1, p2 = prev[SUBLANES - 1:SUBLANES, :], prev[SUBLANES - 2:SUBLANES - 1, :]
    row = lax.broadcasted_iota(jnp.int32, u.shape, 0)
    u1 = jnp.where(row >= 1, pltpu.roll(u, 1, 0), p1)
    u2 = jnp.where(row >= 2, pltpu.roll(u, 2, 0), jnp.where(row == 1, p1, p2))
    cw = cw_ref[...]
    y = cw[0:1, :] * u2 + cw[1:2, :] * u1 + cw[2:3, :] * u
    a_ref[...] = (bg * y).astype(BF16)
    carry_ref[...] = u[rows - SUBLANES:, :]
    cs_ref[...] = u[rows - 2:, :]

    hh_ref[...] = _dot(n, wh_ref[...])
    q_ref[...] = (_dot(n, wq_ref[...]) * SB_SCALE).astype(BF16)
    kt = _dot_nt(wkt_ref[...], n).reshape(SB_HEADS, SB_HEAD_DIM, rows)
    vt = _dot_nt(wvt_ref[...], n).reshape(SB_HEADS, SB_HEAD_DIM, rows)
    kt_ref[...] = kt
    vt_ref[...] = vt
    ktb_ref[...] = kt.astype(BF16)
    vtb_ref[...] = vt.astype(BF16)


def _proj(x, g, wc, wh, wq, wkt, wvt, cw):
    b, t, d = x.shape
    rows = FFN_ROWS

    def tile(w):
        return pl.BlockSpec((None, rows, w), lambda i, j: (i, j, 0))

    kv_tile = pl.BlockSpec((None, SB_HEADS, SB_HEAD_DIM, rows), lambda i, j: (i, 0, 0, j))
    kv_shape = (b, SB_HEADS, SB_HEAD_DIM, t)
    return pl.pallas_call(
        _proj_kernel,
        out_shape=(jax.ShapeDtypeStruct((b, t, D_CONV), BF16),
                   jax.ShapeDtypeStruct((b, t, 4 * D_HGRN), F32),
                   jax.ShapeDtypeStruct((b, t, D_SB), BF16),
                   jax.ShapeDtypeStruct(kv_shape, F32), jax.ShapeDtypeStruct(kv_shape, F32),
                   jax.ShapeDtypeStruct(kv_shape, BF16), jax.ShapeDtypeStruct(kv_shape, BF16),
                   jax.ShapeDtypeStruct((b, 2, D_CONV), F32)),
        grid=(b, t // rows),
        in_specs=[tile(d), _const_spec((1, d)), _const_spec(wc.shape), _const_spec(wh.shape),
                  _const_spec(wq.shape), _const_spec(wkt.shape), _const_spec(wvt.shape),
                  _const_spec(cw.shape)],
        out_specs=(tile(D_CONV), tile(4 * D_HGRN), tile(D_SB), kv_tile, kv_tile, kv_tile, kv_tile,
                   pl.BlockSpec((None, 2, D_CONV), lambda i, j: (i, 0, 0))),
        scratch_shapes=[pltpu.VMEM((SUBLANES, D_CONV), F32)],
        compiler_params=_params("arbitrary", "arbitrary"),
        name="proj",
    )(x, g, wc, wh, wq, wkt, wvt, cw)


def _sample_conv_kernel(bg_ref, cg_ref, xa_ref, prev_ref, cw_ref, a_ref, cs_ref):
    steps = bg_ref.shape[0]
    up = [prev_ref[0], prev_ref[1]] + [cg_ref[t] * xa_ref[t] for t in range(steps)]
    cw = cw_ref[...]
    for t in range(steps):
        y = cw[0:1, :] * up[t] + cw[1:2, :] * up[t + 1] + cw[2:3, :] * up[t + 2]
        a_ref[t] = (bg_ref[t] * y).astype(BF16)
    cs_ref[0] = up[steps]
    cs_ref[1] = up[steps + 1]


def _sample_conv(bg, cg, xa, prev, cw):
    full = lambda s: pl.BlockSpec(s, lambda i: (0,) * len(s))
    return pl.pallas_call(
        _sample_conv_kernel,
        out_shape=(jax.ShapeDtypeStruct(bg.shape, BF16), jax.ShapeDtypeStruct(prev.shape, F32)),
        grid=(1,),
        in_specs=[full(bg.shape), full(cg.shape), full(xa.shape), full(prev.shape), full(cw.shape)],
        out_specs=(full(bg.shape), full(prev.shape)),
        compiler_params=_params("arbitrary"),
        name="sample_conv",
    )(bg, cg, xa, prev, cw)


def _hgrn_kernel(hh_ref, lb_ref, gn_ref, s0_ref, tril_ref, ones_ref, o_ref, sfin_ref,
                 st_ref, b_ref, d_ref, ae_ref, oi_ref, *, chunk, valid):
    heads = D_HGRN // HGRN_HEAD

    @pl.when(pl.program_id(1) == 0)
    def _():
        st_ref[...] = jnp.zeros_like(st_ref)
        for h in range(heads):
            sl = slice(h * HGRN_HEAD, (h + 1) * HGRN_HEAD)
            st_ref[sl, sl] = s0_ref[h]

    q = hh_ref[:, 0:D_HGRN]
    z = hh_ref[:, D_HGRN:2 * D_HGRN]
    v = hh_ref[:, 2 * D_HGRN:3 * D_HGRN]
    g = hh_ref[:, 3 * D_HGRN:4 * D_HGRN]
    lb = lb_ref[...]

    _, log_sig = _log_not_sigmoid(z)
    la = jnp.log(lb)
    lc = jnp.log1p(-lb) + log_sig
    logf = jnp.maximum(la, lc) + jnp.log1p(jnp.exp(-jnp.abs(la - lc)))
    k = (1.0 - lb) * jax.nn.sigmoid(-z)
    srow = lax.broadcasted_iota(jnp.int32, (chunk, 1), 0)
    if valid < chunk:
        logf = jnp.where(srow < valid, logf, 0.0)
        k = jnp.where(srow < valid, k, 0.0)

    tril = tril_ref[...]
    b = sum(_dot(tril, part) for part in _split3(logf))
    b_last = b[chunk - 1:chunk, :]
    b_ref[...] = b

    st = st_ref[...]
    o_inter = _dot_nt((q * jnp.exp(b)).astype(BF16), st.astype(BF16))

    def build(t, carry):
        bt = b_ref[pl.ds(t, 1), :]
        qt = hh_ref[pl.ds(t, 1), 0:D_HGRN]
        dec = jnp.exp(jnp.where(srow <= t, bt - b, -jnp.inf))
        d_ref[pl.ds(pl.multiple_of(t * chunk, chunk), chunk), :] = (qt * k) * dec
        return carry

    lax.fori_loop(0, chunk, build, 0)
    ae_ref[...] = _dot(d_ref[...].astype(BF16), ones_ref[...])

    def apply(t, carry):
        att = ae_ref[pl.ds(pl.multiple_of(t * chunk, chunk), chunk), :]
        oi_ref[pl.ds(t, 1), :] = jnp.sum(att * v, axis=0, keepdims=True)
        return carry

    lax.fori_loop(0, chunk, apply, 0)
    o = o_inter + oi_ref[...]

    head_mask = ones_ref[...].astype(F32)
    kd = k * jnp.exp(b_last - b)
    st_new = st * jnp.exp(b_last) + _dot_tn(v.astype(BF16), kd.astype(BF16)) * head_mask
    st_ref[...] = st_new

    @pl.when(pl.program_id(1) == pl.num_programs(1) - 1)
    def _():
        for h in range(heads):
            sl = slice(h * HGRN_HEAD, (h + 1) * HGRN_HEAD)
            sfin_ref[h] = st_new[sl, sl]

    ms = sum(_dot(part, ones_ref[...]) for part in _split2(o * o)) * (1.0 / HGRN_HEAD)
    y = o * lax.rsqrt(ms + RMS_EPS) * gn_ref[...]
    o_ref[...] = (y * (g * jax.nn.sigmoid(g))).astype(BF16)


def _hgrn(hh, lb, gn, s0t, chunk, valid):
    bsz, t, _ = hh.shape
    heads = D_HGRN // HGRN_HEAD
    r = jnp.arange(chunk)
    tril = (r[:, None] >= r[None, :]).astype(BF16)
    hd = jnp.arange(D_HGRN) // HGRN_HEAD
    ones = (hd[:, None] == hd[None, :]).astype(BF16)
    st_spec = pl.BlockSpec((None, heads, HGRN_HEAD, HGRN_HEAD), lambda i, j: (i, 0, 0, 0))
    return pl.pallas_call(
        functools.partial(_hgrn_kernel, chunk=chunk, valid=valid),
        out_shape=(jax.ShapeDtypeStruct((bsz, t, D_HGRN), BF16),
                   jax.ShapeDtypeStruct(s0t.shape, F32)),
        grid=(bsz, t // chunk),
        in_specs=[pl.BlockSpec((None, chunk, 4 * D_HGRN), lambda i, j: (i, j, 0)),
                  _const_spec((1, D_HGRN)), _const_spec((1, D_HGRN)), st_spec,
                  _const_spec(tril.shape), _const_spec(ones.shape)],
        out_specs=(pl.BlockSpec((None, chunk, D_HGRN), lambda i, j: (i, j, 0)), st_spec),
        scratch_shapes=[pltpu.VMEM((D_HGRN, D_HGRN), F32),
                        pltpu.VMEM((chunk, D_HGRN), F32),
                        pltpu.VMEM((chunk * chunk, D_HGRN), F32),
                        pltpu.VMEM((chunk * chunk, D_HGRN), F32),
                        pltpu.VMEM((chunk, D_HGRN), F32)],
        compiler_params=_params("arbitrary", "arbitrary"),
        name="hgrn",
    )(hh, lb, gn, s0t, tril, ones)


def _suffix_ones(n):
    r = jnp.arange(n)
    return (r[:, None] > r[None, :]).astype(BF16)


def _attn_kernel(bias_ref, q_ref, kt_ref, vt_ref, u_ref, o_ref, acc_ref, car_ref, *, tq, tk):
    hp = pl.program_id(1)
    i = pl.program_id(2)
    heads = q_ref.shape[1] // SB_HEAD_DIM
    acc_ref[...] = jnp.zeros_like(acc_ref)
    car_ref[...] = jnp.zeros_like(car_ref)
    qs = [q_ref[:, h * SB_HEAD_DIM:(h + 1) * SB_HEAD_DIM] for h in range(heads)]
    bias = [bias_ref[hp * heads + h] for h in range(heads)]
    u = u_ref[...]

    def block(j, masked):
        col = pl.ds(pl.multiple_of(j * tk, tk), tk)
        for h in range(heads):
            z = _dot(qs[h], kt_ref[h, :, col]) + bias[h]
            log_not, log_sig = _log_not_sigmoid(z)
            if masked:
                keep = (lax.broadcasted_iota(jnp.int32, z.shape, 1)
                        < lax.broadcasted_iota(jnp.int32, z.shape, 0))
                log_not = jnp.where(keep, log_not, 0.0)
            after = _dot(log_not.astype(BF16), u) + car_ref[h]
            w = jnp.exp(log_sig + after)
            if masked:
                w = jnp.where(keep, w, 0.0)
            acc_ref[h] += _dot_nt(w.astype(BF16), vt_ref[h, :, col])
            car_ref[h] += jnp.sum(log_not, axis=1, keepdims=True)

    block(i, True)

    def body(jj, carry):
        block(i - 1 - jj, False)
        return carry

    lax.fori_loop(0, i, body, 0)
    o_ref[...] = jnp.concatenate([acc_ref[h] for h in range(heads)], axis=-1).astype(BF16)


def _attn(q, ktb, vtb, bias):
    b, t, _ = q.shape
    tq = tk = min(ATT_Q, t)
    pair = LANES // SB_HEAD_DIM
    kv_spec = pl.BlockSpec((None, pair, SB_HEAD_DIM, t), lambda bi, hp, i: (bi, hp, 0, 0))
    qo_spec = pl.BlockSpec((None, tq, LANES), lambda bi, hp, i: (bi, i, hp))
    return pl.pallas_call(
        functools.partial(_attn_kernel, tq=tq, tk=tk),
        out_shape=jax.ShapeDtypeStruct(q.shape, BF16),
        grid=(b, SB_HEADS // pair, t // tq),
        in_specs=[_smem_spec(), qo_spec, kv_spec, kv_spec, _const_spec((tk, tk))],
        out_specs=qo_spec,
        scratch_shapes=[pltpu.VMEM((pair, tq, SB_HEAD_DIM), F32),
                        pltpu.VMEM((pair, tq, 1), F32)],
        compiler_params=_params("parallel", "parallel", "arbitrary"),
        name="attn",
    )(bias, q, ktb, vtb, _suffix_ones(tk))


def _paged_attn_kernel(pt_ref, bias_ref, q_ref, knew_ref, vnew_ref, u_ref, ck_ref, cv_ref, o_ref,
                       kbuf, vbuf, sem, acc_ref, car_ref, *, layer, pages, group):
    s = pl.program_id(0)
    nseq = pl.num_programs(0)
    chunks = pages // group

    def copies(seq, c, slot):
        out = []
        for g in range(group):
            page = pt_ref[seq * pages + (chunks - 1 - c) * group + g]
            out.append(pltpu.make_async_copy(ck_ref.at[layer, page], kbuf.at[slot, g], sem.at[0, slot]))
            out.append(pltpu.make_async_copy(cv_ref.at[layer, page], vbuf.at[slot, g], sem.at[1, slot]))
        return out

    def start(seq, c, slot):
        for cp in copies(seq, c, slot):
            cp.start()

    def wait(seq, c, slot):
        for cp in copies(seq, c, slot):
            cp.wait()

    @pl.when(s == 0)
    def _():
        start(0, 0, 0)

    acc_ref[...] = jnp.zeros_like(acc_ref)
    car_ref[...] = jnp.zeros_like(car_ref)
    q = (q_ref[...] * SB_SCALE).astype(BF16)
    u = u_ref[...]

    def page_block(kt_of, vt_of, keep):
        z = jnp.concatenate(
            [_dot(q[h], kt_of(h).astype(BF16)) + bias_ref[h] for h in range(SB_HEADS)], axis=0)
        log_not, log_sig = _log_not_sigmoid(z)
        if keep is not None:
            log_not = jnp.where(keep, log_not, 0.0)
        after = sum(_dot(part, u) for part in _split2(log_not)) + car_ref[...]
        w = jnp.exp(log_sig + after)
        if keep is not None:
            w = jnp.where(keep, w, 0.0)
        wb = w.astype(BF16)
        for h in range(SB_HEADS):
            acc_ref[h] += _dot_nt(wb[h * SUBLANES:(h + 1) * SUBLANES], vt_of(h).astype(BF16))
        car_ref[...] += jnp.sum(log_not, axis=1, keepdims=True)

    shape = (SB_HEADS * SUBLANES, LANES)
    step = lax.broadcasted_iota(jnp.int32, shape, 0) % SUBLANES
    keep_new = lax.broadcasted_iota(jnp.int32, shape, 1) < step
    page_block(lambda h: knew_ref[h], lambda h: vnew_ref[h], keep_new)

    def chunk_body(c, carry):
        slot = c % 2

        @pl.when(c + 1 < chunks)
        def _():
            start(s, c + 1, 1 - slot)

        @pl.when(jnp.logical_and(c + 1 == chunks, s + 1 < nseq))
        def _():
            start(s + 1, 0, 1 - slot)

        wait(s, c, slot)

        def one_page(gg, carry2):
            g = group - 1 - gg
            page_block(lambda h: kbuf[slot, g, h], lambda h: vbuf[slot, g, h], None)
            return carry2

        lax.fori_loop(0, group, one_page, 0)
        return carry

    lax.fori_loop(0, chunks, chunk_body, 0)
    o_ref[...] = acc_ref[...]


def _paged_attn(page_table, bias, q, knew, vnew, ck, cv, layer):
    nseq, pages = page_table.shape
    page = ck.shape[-1]
    group = DEC_PAGES
    while pages % (2 * group):
        group //= 2
    row = lambda shp: pl.BlockSpec((None,) + shp, lambda i, *_: (i,) + (0,) * len(shp))
    grid_spec = pltpu.PrefetchScalarGridSpec(
        num_scalar_prefetch=1,
        grid=(nseq,),
        in_specs=[_smem_spec(), row(q.shape[1:]), row(knew.shape[1:]), row(vnew.shape[1:]),
                  pl.BlockSpec((page, page), lambda i, *_: (0, 0)),
                  pl.BlockSpec(memory_space=pl.ANY), pl.BlockSpec(memory_space=pl.ANY)],
        out_specs=row(q.shape[1:]),
        scratch_shapes=[pltpu.VMEM((2, group, SB_HEADS, SB_HEAD_DIM, page), F32),
                        pltpu.VMEM((2, group, SB_HEADS, SB_HEAD_DIM, page), F32),
                        pltpu.SemaphoreType.DMA((2, 2)),
                        pltpu.VMEM((SB_HEADS, SUBLANES, SB_HEAD_DIM), F32),
                        pltpu.VMEM((SB_HEADS * SUBLANES, 1), F32)],
    )
    return pl.pallas_call(
        functools.partial(_paged_attn_kernel, layer=layer, pages=pages, group=group),
        out_shape=jax.ShapeDtypeStruct(q.shape, F32),
        grid_spec=grid_spec,
        compiler_params=_params("arbitrary"),
        name="paged_attn",
    )(page_table.reshape(-1), bias, q, knew, vnew, _suffix_ones(page), ck, cv)


def _lower_bounds(lb_logits):
    p = jax.nn.softmax(lb_logits.astype(F32), axis=0)
    c = jnp.cumsum(p, axis=0)
    return c - c[0:1]


def kernel(x_prompt, x_sample, state_conv, state_hgrn, cache_k, cache_v, page_table, norm_ffn1, w_ffn1_gate, w_ffn1_up, w_ffn1_down, norm_mix, w_in, conv_w, hgrn_lb_logits, hgrn_gnorm, sb_bias, w_out, norm_ffn2, w_ffn2_gate, w_ffn2_up, w_ffn2_down, norm_final):
    depth = w_in.shape[0]
    bp, tp, d = x_prompt.shape
    bs, ts, _ = x_sample.shape
    heads_h = D_HGRN // HGRN_HEAD
    lower = _lower_bounds(hgrn_lb_logits)
    ck = jnp.transpose(cache_k, (0, 1, 3, 4, 2))
    cv = jnp.transpose(cache_v, (0, 1, 3, 4, 2))
    page = ck.shape[-1]
    bf = lambda w: w.astype(BF16)
    row = lambda v: v.reshape(1, -1)
    o_conv, o_hgrn, o_q = 0, 3 * D_CONV, 3 * D_CONV + 4 * D_HGRN
    o_k, o_v = o_q + D_SB, o_q + 2 * D_SB

    xp = x_prompt.reshape(bp * tp, d)
    xs = x_sample.reshape(bs * ts, d)
    conv_p, conv_s, hg_p, hg_s, kp_l, vp_l, ks_l, vs_l = [], [], [], [], [], [], [], []
    for l in range(depth):
        final = l == depth - 1
        w1 = (row(norm_ffn1[l]), bf(w_ffn1_gate[l]), bf(w_ffn1_up[l]), bf(w_ffn1_down[l]))
        w2 = (row(norm_ffn2[l]), bf(w_ffn2_gate[l]), bf(w_ffn2_up[l]), bf(w_ffn2_down[l]))
        wi = bf(w_in[l])
        wo = bf(w_out[l])
        gmix, gfin = row(norm_mix[l]), row(norm_final)
        lb, gn = row(lower[l]), row(hgrn_gnorm[l])

        xp = _ffn(xp, *w1)
        a_out, hh, q, kt, vt, ktb, vtb, cst = _proj(
            xp.reshape(bp, tp, d), gmix, wi[:, o_conv:o_hgrn], wi[:, o_hgrn:o_q], wi[:, o_q:o_k],
            wi[:, o_k:o_v].T, wi[:, o_v:].T, conv_w[l])
        b_out, sfin = _hgrn(hh, lb, gn, jnp.zeros((bp, heads_h, HGRN_HEAD, HGRN_HEAD), F32),
                            HGRN_CHUNK, HGRN_CHUNK)
        c_out = _attn(q, ktb, vtb, sb_bias[l])
        xp = _out_ffn(xp, a_out.reshape(bp * tp, -1), b_out.reshape(bp * tp, -1),
                      c_out.reshape(bp * tp, -1), wo, *w2, gfin, final)
        conv_p.append(cst)
        hg_p.append(jnp.swapaxes(sfin, -1, -2))
        kp_l.append(jnp.transpose(kt, (0, 3, 1, 2)))
        vp_l.append(jnp.transpose(vt, (0, 3, 1, 2)))

        xs = _ffn(xs, *w1)
        hs = _norm_matmul(xs, gmix, wi)
        tm = lambda c0, w: jnp.swapaxes(hs[:, c0:c0 + w].reshape(bs, ts, w), 0, 1)
        a_s, cs_new = _sample_conv(tm(0, D_CONV), tm(D_CONV, D_CONV), tm(2 * D_CONV, D_CONV),
                                   jnp.swapaxes(state_conv[l], 0, 1), conv_w[l])
        hh_s = jnp.pad(hs[:, o_hgrn:o_q].reshape(bs, ts, -1), ((0, 0), (0, SUBLANES - ts), (0, 0)))
        b_s, sfin_s = _hgrn(hh_s, lb, gn, jnp.swapaxes(state_hgrn[l], -1, -2), SUBLANES, ts)
        heads4 = lambda c0: hs[:, c0:c0 + D_SB].reshape(bs, ts, SB_HEADS, SB_HEAD_DIM)
        q_s = jnp.pad(jnp.swapaxes(heads4(o_q), 1, 2), ((0, 0), (0, 0), (0, SUBLANES - ts), (0, 0)))
        k_s, v_s = heads4(o_k), heads4(o_v)
        new_t = lambda a: jnp.pad(jnp.transpose(a, (0, 2, 3, 1)),
                                  ((0, 0), (0, 0), (0, 0), (0, page - ts)))
        c_s = _paged_attn(page_table, sb_bias[l], q_s, new_t(k_s), new_t(v_s), ck, cv, l)
        c_s = jnp.swapaxes(c_s[:, :, :ts, :], 1, 2).reshape(bs * ts, D_SB).astype(BF16)
        xs = _out_ffn(xs, jnp.swapaxes(a_s, 0, 1).reshape(bs * ts, -1),
                      b_s[:, :ts, :].reshape(bs * ts, -1), c_s, wo, *w2, gfin, final)
        conv_s.append(jnp.swapaxes(cs_new, 0, 1))
        hg_s.append(jnp.swapaxes(sfin_s, -1, -2))
        ks_l.append(k_s)
        vs_l.append(v_s)

    return (xp.reshape(bp, tp, d), xs.reshape(bs, ts, d), jnp.stack(conv_p), jnp.stack(conv_s),
            jnp.stack(hg_p), jnp.stack(hg_s), jnp.stack(kp_l), jnp.stack(vp_l),
            jnp.stack(ks_l), jnp.stack(vs_l))
```

```python
import functools
import math

import jax
import jax.numpy as jnp
from jax import lax
from jax.experimental import pallas as pl
from jax.experimental.pallas import tpu as pltpu

F32 = jnp.float32
BF16 = jnp.bfloat16

RMS_EPS = 1e-6
D_CONV = 256
D_HGRN = 256
HGRN_HEAD = 64
D_SB = 512
SB_HEADS = 8
SB_HEAD_DIM = 64
SB_SCALE = SB_HEAD_DIM ** -0.5
LOG2E = math.log2(math.e)

V7X_VMEM_BYTES = 64 * 1024 * 1024
VMEM_LIMIT = V7X_VMEM_BYTES - 8 * 1024 * 1024
LANES = 128
SUBLANES = 8

FFN_ROWS = 256
HGRN_CHUNK = 64
ATT_Q = 256
DEC_PAGES = 16


def _const_spec(shape):
    nd = len(shape)
    return pl.BlockSpec(shape, lambda *_: (0,) * nd, pipeline_mode=pl.Buffered(1))


def _smem_spec():
    return pl.BlockSpec(memory_space=pltpu.SMEM)


def _params(*sem):
    return pltpu.CompilerParams(dimension_semantics=sem, vmem_limit_bytes=VMEM_LIMIT)


def _rms(x, g):
    return x * lax.rsqrt(jnp.mean(x * x, axis=-1, keepdims=True) + RMS_EPS) * g


def _split2(x):
    hi = x.astype(BF16)
    lo = (x - hi.astype(F32)).astype(BF16)
    return hi, lo


def _split3(x):
    hi = x.astype(BF16)
    r = x - hi.astype(F32)
    mid = r.astype(BF16)
    lo = (r - mid.astype(F32)).astype(BF16)
    return hi, mid, lo


def _dot(a, b):
    return jnp.dot(a, b, preferred_element_type=F32)


def _dot_nt(a, b):
    return lax.dot_general(a, b, (((1,), (1,)), ((), ())), preferred_element_type=F32)


def _dot_tn(a, b):
    return lax.dot_general(a, b, (((0,), (0,)), ((), ())), preferred_element_type=F32)


def _log_not_sigmoid(z):
    nz = -z
    l1p = jnp.log1p(jnp.exp(jnp.minimum(z, nz)))
    log_not = jnp.minimum(nz, 0.0) - l1p
    return log_not, z + log_not


def _swiglu_residual(x, g, wg_ref, wu_ref, wd_ref):
    n = _rms(x, g).astype(BF16)
    gate = _dot(n, wg_ref[...])
    up = _dot(n, wu_ref[...])
    act = (gate * jax.nn.sigmoid(gate) * up).astype(BF16)
    return x + 0.5 * _dot(act, wd_ref[...])


def _ffn_kernel(x_ref, g_ref, wg_ref, wu_ref, wd_ref, o_ref):
    o_ref[...] = _swiglu_residual(x_ref[...], g_ref[...], wg_ref, wu_ref, wd_ref)


def _ffn(x, g, wg, wu, wd):
    n, d = x.shape
    rows = min(FFN_ROWS, n)
    tile = pl.BlockSpec((rows, d), lambda i: (i, 0))
    return pl.pallas_call(
        _ffn_kernel,
        out_shape=jax.ShapeDtypeStruct((n, d), F32),
        grid=(n // rows,),
        in_specs=[tile, _const_spec((1, d)), _const_spec(wg.shape), _const_spec(wu.shape),
                  _const_spec(wd.shape)],
        out_specs=tile,
        compiler_params=_params("parallel"),
        name="ffn",
    )(x, g, wg, wu, wd)


def _out_ffn_kernel(x_ref, a_ref, b_ref, c_ref, wo_ref, g_ref, wg_ref, wu_ref, wd_ref, gf_ref,
                    o_ref, *, final):
    mix = jnp.concatenate([a_ref[...], b_ref[...], c_ref[...]], axis=-1)
    x = x_ref[...] + _dot(mix, wo_ref[...])
    x = _swiglu_residual(x, g_ref[...], wg_ref, wu_ref, wd_ref)
    o_ref[...] = _rms(x, gf_ref[...]) if final else x


def _out_ffn(x, a, b, c, wo, g, wg, wu, wd, gf, final):
    n, d = x.shape
    rows = min(FFN_ROWS, n)

    def tile(w):
        return pl.BlockSpec((rows, w), lambda i: (i, 0))

    return pl.pallas_call(
        functools.partial(_out_ffn_kernel, final=final),
        out_shape=jax.ShapeDtypeStruct((n, d), F32),
        grid=(n // rows,),
        in_specs=[tile(d), tile(a.shape[1]), tile(b.shape[1]), tile(c.shape[1]),
                  _const_spec(wo.shape), _const_spec((1, d)), _const_spec(wg.shape),
                  _const_spec(wu.shape), _const_spec(wd.shape), _const_spec((1, d))],
        out_specs=tile(d),
        compiler_params=_params("parallel"),
        name="out_ffn",
    )(x, a, b, c, wo, g, wg, wu, wd, gf)


def _norm_matmul_kernel(x_ref, g_ref, w_ref, o_ref):
    o_ref[...] = _dot(_rms(x_ref[...], g_ref[...]).astype(BF16), w_ref[...])


def _norm_matmul(x, g, w):
    n, d = x.shape
    return pl.pallas_call(
        _norm_matmul_kernel,
        out_shape=jax.ShapeDtypeStruct((n, w.shape[1]), F32),
        grid=(1,),
        in_specs=[_const_spec((n, d)), _const_spec((1, d)), _const_spec(w.shape)],
        out_specs=pl.BlockSpec((n, w.shape[1]), lambda i: (0, 0)),
        compiler_params=_params("arbitrary"),
        name="norm_matmul",
    )(x, g, w)


def _proj_kernel(x_ref, g_ref, wc_ref, wh_ref, wq_ref, wkt_ref, wvt_ref, cw_ref,
                 a_ref, hh_ref, q_ref, kt_ref, vt_ref, ktb_ref, vtb_ref, cs_ref, carry_ref):
    @pl.when(pl.program_id(1) == 0)
    def _():
        carry_ref[...] = jnp.zeros_like(carry_ref)

    n = _rms(x_ref[...], g_ref[...]).astype(BF16)
    rows = n.shape[0]

    hc = _dot(n, wc_ref[...])
    bg, cg, xa = hc[:, :D_CONV], hc[:, D_CONV:2 * D_CONV], hc[:, 2 * D_CONV:]
    u = cg * xa
    prev = carry_ref[...]
    p1, p2 = prev[SUBLANES - 1:SUBLANES, :], prev[SUBLANES - 2:SUBLANES - 1, :]
    row = lax.broadcasted_iota(jnp.int32, u.shape, 0)
    u1 = jnp.where(row >= 1, pltpu.roll(u, 1, 0), p1)
    u2 = jnp.where(row >= 2, pltpu.roll(u, 2, 0), jnp.where(row == 1, p1, p2))
    cw = cw_ref[...]
    y = cw[0:1, :] * u2 + cw[1:2, :] * u1 + cw[2:3, :] * u
    a_ref[...] = (bg * y).astype(BF16)
    carry_ref[...] = u[rows - SUBLANES:, :]
    cs_ref[...] = u[rows - 2:, :]

    hh_ref[...] = _dot(n, wh_ref[...])
    q_ref[...] = (_dot(n, wq_ref[...]) * (SB_SCALE * LOG2E)).astype(BF16)
    kt = _dot_nt(wkt_ref[...], n).reshape(SB_HEADS, SB_HEAD_DIM, rows)
    vt = _dot_nt(wvt_ref[...], n).reshape(SB_HEADS, SB_HEAD_DIM, rows)
    kt_ref[...] = kt
    vt_ref[...] = vt
    ktb_ref[...] = kt.astype(BF16)
    vtb_ref[...] = vt.astype(BF16)


def _proj(x, g, wc, wh, wq, wkt, wvt, cw):
    b, t, d = x.shape
    rows = FFN_ROWS

    def tile(w):
        return pl.BlockSpec((None, rows, w), lambda i, j: (i, j, 0))

    kv_tile = pl.BlockSpec((None, SB_HEADS, SB_HEAD_DIM, rows), lambda i, j: (i, 0, 0, j))
    kv_shape = (b, SB_HEADS, SB_HEAD_DIM, t)
    return pl.pallas_call(
        _proj_kernel,
        out_shape=(jax.ShapeDtypeStruct((b, t, D_CONV), BF16),
                   jax.ShapeDtypeStruct((b, t, 4 * D_HGRN), F32),
                   jax.ShapeDtypeStruct((b, t, D_SB), BF16),
                   jax.ShapeDtypeStruct(kv_shape, F32), jax.ShapeDtypeStruct(kv_shape, F32),
                   jax.ShapeDtypeStruct(kv_shape, BF16), jax.ShapeDtypeStruct(kv_shape, BF16),
                   jax.ShapeDtypeStruct((b, 2, D_CONV), F32)),
        grid=(b, t // rows),
        in_specs=[tile(d), _const_spec((1, d)), _const_spec(wc.shape), _const_spec(wh.shape),
                  _const_spec(wq.shape), _const_spec(wkt.shape), _const_spec(wvt.shape),
                  _const_spec(cw.shape)],
        out_specs=(tile(D_CONV), tile(4 * D_HGRN), tile(D_SB), kv_tile, kv_tile, kv_tile, kv_tile,
                   pl.BlockSpec((None, 2, D_CONV), lambda i, j: (i, 0, 0))),
        scratch_shapes=[pltpu.VMEM((SUBLANES, D_CONV), F32)],
        compiler_params=_params("arbitrary", "arbitrary"),
        name="proj",
    )(x, g, wc, wh, wq, wkt, wvt, cw)


def _sample_conv_kernel(bg_ref, cg_ref, xa_ref, prev_ref, cw_ref, a_ref, cs_ref):
    steps = bg_ref.shape[0]
    up = [prev_ref[0], prev_ref[1]] + [cg_ref[t] * xa_ref[t] for t in range(steps)]
    cw = cw_ref[...]
    for t in range(steps):
        y = cw[0:1, :] * up[t] + cw[1:2, :] * up[t + 1] + cw[2:3, :] * up[t + 2]
        a_ref[t] = (bg_ref[t] * y).astype(BF16)
    cs_ref[0] = up[steps]
    cs_ref[1] = up[steps + 1]


def _sample_conv(bg, cg, xa, prev, cw):
    full = lambda s: pl.BlockSpec(s, lambda i: (0,) * len(s))
    return pl.pallas_call(
        _sample_conv_kernel,
        out_shape=(jax.ShapeDtypeStruct(bg.shape, BF16), jax.ShapeDtypeStruct(prev.shape, F32)),
        grid=(1,),
        in_specs=[full(bg.shape), full(cg.shape), full(xa.shape), full(prev.shape), full(cw.shape)],
        out_specs=(full(bg.shape), full(prev.shape)),
        compiler_params=_params("arbitrary"),
        name="sample_conv",
    )(bg, cg, xa, prev, cw)


def _hgrn_kernel(hh_ref, lb_ref, gn_ref, s0_ref, tril_ref, ones_ref, o_ref, sfin_ref,
                 st_ref, b_ref, d_ref, ae_ref, oi_ref, *, chunk, valid):
    heads = D_HGRN // HGRN_HEAD

    @pl.when(pl.program_id(1) == 0)
    def _():
        st_ref[...] = jnp.zeros_like(st_ref)
        for h in range(heads):
            sl = slice(h * HGRN_HEAD, (h + 1) * HGRN_HEAD)
            st_ref[sl, sl] = s0_ref[h]

    q = hh_ref[:, 0:D_HGRN]
    z = hh_ref[:, D_HGRN:2 * D_HGRN]
    v = hh_ref[:, 2 * D_HGRN:3 * D_HGRN]
    g = hh_ref[:, 3 * D_HGRN:4 * D_HGRN]
    lb = lb_ref[...]

    _, log_sig = _log_not_sigmoid(z)
    la = jnp.log(lb)
    lc = jnp.log1p(-lb) + log_sig
    logf = jnp.maximum(la, lc) + jnp.log1p(jnp.exp(-jnp.abs(la - lc)))
    k = (1.0 - lb) * jax.nn.sigmoid(-z)
    srow = lax.broadcasted_iota(jnp.int32, (chunk, 1), 0)
    if valid < chunk:
        logf = jnp.where(srow < valid, logf, 0.0)
        k = jnp.where(srow < valid, k, 0.0)

    tril = tril_ref[...]
    b = sum(_dot(tril, part) for part in _split3(logf))
    b_last = b[chunk - 1:chunk, :]
    b_ref[...] = b

    st = st_ref[...]
    o_inter = _dot_nt((q * jnp.exp(b)).astype(BF16), st.astype(BF16))

    tiles = chunk // SUBLANES
    half_tile = SUBLANES * SUBLANES // 2

    def pair_rows(i, r):
        n = SUBLANES * (i + 1)
        return pl.ds(pl.multiple_of(half_tile * i * (i + 1) + r * n, SUBLANES), n)

    for i in range(tiles):
        n = SUBLANES * (i + 1)

        def build(r, carry, i=i, n=n):
            t = i * SUBLANES + r
            bt = b_ref[pl.ds(t, 1), :]
            qt = hh_ref[pl.ds(t, 1), 0:D_HGRN]
            dec = jnp.exp(jnp.where(srow[:n] <= t, bt - b[:n], -jnp.inf))
            d_ref[pair_rows(i, r), :] = (qt * k[:n]) * dec
            return carry

        lax.fori_loop(0, SUBLANES, build, 0, unroll=True)
    ae_ref[...] = _dot(d_ref[...].astype(BF16), ones_ref[...])

    for i in range(tiles):
        n = SUBLANES * (i + 1)

        def apply(r, carry, i=i, n=n):
            oi_ref[pl.ds(i * SUBLANES + r, 1), :] = jnp.sum(ae_ref[pair_rows(i, r), :] * v[:n], axis=0,
                                                           keepdims=True)
            return carry

        lax.fori_loop(0, SUBLANES, apply, 0, unroll=True)
    o = o_inter + oi_ref[...]

    head_mask = ones_ref[...].astype(F32)
    kd = k * jnp.exp(b_last - b)
    st_new = st * jnp.exp(b_last) + _dot_tn(v.astype(BF16), kd.astype(BF16)) * head_mask
    st_ref[...] = st_new

    @pl.when(pl.program_id(1) == pl.num_programs(1) - 1)
    def _():
        for h in range(heads):
            sl = slice(h * HGRN_HEAD, (h + 1) * HGRN_HEAD)
            sfin_ref[h] = st_new[sl, sl]

    ms = sum(_dot(part, ones_ref[...]) for part in _split2(o * o)) * (1.0 / HGRN_HEAD)
    y = o * lax.rsqrt(ms + RMS_EPS) * gn_ref[...]
    o_ref[...] = (y * (g * jax.nn.sigmoid(g))).astype(BF16)


def _hgrn(hh, lb, gn, s0t, chunk, valid):
    bsz, t, _ = hh.shape
    heads = D_HGRN // HGRN_HEAD
    r = jnp.arange(chunk)
    tril = (r[:, None] >= r[None, :]).astype(BF16)
    hd = jnp.arange(D_HGRN) // HGRN_HEAD
    ones = (hd[:, None] == hd[None, :]).astype(BF16)
    st_spec = pl.BlockSpec((None, heads, HGRN_HEAD, HGRN_HEAD), lambda i, j: (i, 0, 0, 0))
    tiles = chunk // SUBLANES
    pairs = (SUBLANES * SUBLANES // 2) * tiles * (tiles + 1)
    return pl.pallas_call(
        functools.partial(_hgrn_kernel, chunk=chunk, valid=valid),
        out_shape=(jax.ShapeDtypeStruct((bsz, t, D_HGRN), BF16),
                   jax.ShapeDtypeStruct(s0t.shape, F32)),
        grid=(bsz, t // chunk),
        in_specs=[pl.BlockSpec((None, chunk, 4 * D_HGRN), lambda i, j: (i, j, 0)),
                  _const_spec((1, D_HGRN)), _const_spec((1, D_HGRN)), st_spec,
                  _const_spec(tril.shape), _const_spec(ones.shape)],
        out_specs=(pl.BlockSpec((None, chunk, D_HGRN), lambda i, j: (i, j, 0)), st_spec),
        scratch_shapes=[pltpu.VMEM((D_HGRN, D_HGRN), F32),
                        pltpu.VMEM((chunk, D_HGRN), F32),
                        pltpu.VMEM((pairs, D_HGRN), F32),
                        pltpu.VMEM((pairs, D_HGRN), F32),
                        pltpu.VMEM((chunk, D_HGRN), F32)],
        compiler_params=_params("arbitrary", "arbitrary"),
        name="hgrn",
    )(hh, lb, gn, s0t, tril, ones)


def _suffix_ones(n):
    r = jnp.arange(n)
    return (r[:, None] > r[None, :]).astype(BF16)


def _suffix_and_total(n):
    return jnp.concatenate([_suffix_ones(n), jnp.ones((n, n), BF16)], axis=1)


def _neg_log2_not_sigmoid(z):
    sign = jnp.int32(-2 ** 31)
    neg_abs = lax.bitcast_convert_type(lax.bitcast_convert_type(z, jnp.int32) | sign, F32)
    p = jnp.maximum(z, 0.0) + jnp.log(1.0 + jnp.exp2(neg_abs)) * LOG2E
    return p, z - p


def _attn_kernel(bias_ref, q_ref, kt_ref, vt_ref, u_ref, o_ref, acc_ref, car_ref, ls_ref, r_ref,
                 z_ref, *, tq, tk):
    i = pl.program_id(1)
    heads = q_ref.shape[1] // SB_HEAD_DIM
    bias = [bias_ref[h] * LOG2E for h in range(heads)]
    lane_tiles = tk // LANES

    def col(j):
        return pl.ds(pl.multiple_of(j * tk, tk), tk)

    def scores(h, j):
        z_ref[h] = _dot(q_ref[:, h * SB_HEAD_DIM:(h + 1) * SB_HEAD_DIM], kt_ref[h, :, col(j)])

    def logs(h, keep):
        p, log_sig = _neg_log2_not_sigmoid(z_ref[h] + bias[h])
        if keep is not None:
            p = jnp.where(keep, p, 0.0)
            log_sig = jnp.where(keep, log_sig, -jnp.inf)
        ls_ref[h] = log_sig
        r_ref[h] = _dot(p.astype(BF16), u_ref[...])
        return jnp.broadcast_to(jnp.sum(p, axis=1, keepdims=True), (tq, LANES))

    def weights(h, j):
        after = r_ref[h] + jnp.concatenate([car_ref[h]] * lane_tiles, axis=1)
        w = jnp.exp2(ls_ref[h] - after)
        acc_ref[h] += _dot_nt(w.astype(BF16), vt_ref[h, :, col(j)])

    keep = (lax.broadcasted_iota(jnp.int32, (tq, tk), 1) < lax.broadcasted_iota(jnp.int32, (tq, tk), 0))
    for h in range(heads):
        scores(h, i)
    for h in range(heads):
        car_ref[h] = jnp.zeros((tq, LANES), F32)
        car_ref[heads + h] = logs(h, keep)
        acc_ref[h] = jnp.zeros((tq, SB_HEAD_DIM), F32)
    for h in range(heads):
        scores(h, jnp.maximum(i - 1, 0))

    def body(jj, carry):
        j = i - 1 - jj
        for h in range(heads):
            weights(h, j + 1)
        for h in range(heads):
            total = logs(h, None)
            c = car_ref[heads + h]
            car_ref[h] = c
            car_ref[heads + h] = c + total
        for h in range(heads):
            scores(h, jnp.maximum(j - 1, 0))
        return carry

    lax.fori_loop(0, i, body, 0)
    for h in range(heads):
        weights(h, 0)
    o_ref[...] = jnp.concatenate([acc_ref[h] for h in range(heads)], axis=-1).astype(BF16)


def _attn(q, ktb, vtb, bias):
    b, t, _ = q.shape
    tq = tk = min(ATT_Q, t)
    kv_spec = pl.BlockSpec((None, SB_HEADS, SB_HEAD_DIM, t), lambda bi, i: (bi, 0, 0, 0),
                           pipeline_mode=pl.Buffered(1))
    qo_spec = pl.BlockSpec((None, tq, D_SB), lambda bi, i: (bi, i, 0))
    return pl.pallas_call(
        functools.partial(_attn_kernel, tq=tq, tk=tk),
        out_shape=jax.ShapeDtypeStruct(q.shape, BF16),
        grid=(b, t // tq),
        in_specs=[_smem_spec(), qo_spec, kv_spec, kv_spec, _const_spec((tk, tk))],
        out_specs=qo_spec,
        scratch_shapes=[pltpu.VMEM((SB_HEADS, tq, SB_HEAD_DIM), F32),
                        pltpu.VMEM((2 * SB_HEADS, tq, LANES), F32),
                        pltpu.VMEM((SB_HEADS, tq, tk), F32),
                        pltpu.VMEM((SB_HEADS, tq, tk), F32),
                        pltpu.VMEM((SB_HEADS, tq, tk), F32)],
        compiler_params=_params("parallel", "arbitrary"),
        name="attn",
    )(bias, q, ktb, vtb, _suffix_ones(tk))


def _paged_attn_kernel(pt_ref, bias_ref, q_ref, knew_ref, vnew_ref, ue_ref, ck_ref, cv_ref, o_ref,
                       kbuf, vbuf, sem, acc_ref, car_ref, *, layer, pages, group):
    s = pl.program_id(0)
    nseq = pl.num_programs(0)
    chunks = pages // group
    rows = SB_HEADS * SUBLANES

    def copies(seq, c, slot):
        out = []
        for g in range(group):
            page = pt_ref[seq * pages + (chunks - 1 - c) * group + g]
            lanes = pl.ds(g * LANES, LANES)
            out.append(pltpu.make_async_copy(ck_ref.at[layer, page], kbuf.at[slot, :, :, lanes],
                                             sem.at[0, slot]))
            out.append(pltpu.make_async_copy(cv_ref.at[layer, page], vbuf.at[slot, :, :, lanes],
                                             sem.at[1, slot]))
        return out

    def start(seq, c, slot):
        for cp in copies(seq, c, slot):
            cp.start()

    def wait(seq, c, slot):
        for cp in copies(seq, c, slot):
            cp.wait()

    @pl.when(s == 0)
    def _():
        start(0, 0, 0)

    acc_ref[...] = jnp.zeros_like(acc_ref)
    car_ref[...] = jnp.zeros_like(car_ref)
    q = (q_ref[...] * (SB_SCALE * LOG2E)).astype(BF16)
    bias = [bias_ref[h] * LOG2E for h in range(SB_HEADS)]

    def block(kt_of, vt_of, npages, keep):
        z = jnp.concatenate(
            [_dot(q[h], kt_of(h).astype(BF16)) + bias[h] for h in range(SB_HEADS)], axis=0)
        p, log_sig = _neg_log2_not_sigmoid(z)
        if keep is not None:
            p = jnp.where(keep, p, 0.0)
            log_sig = jnp.where(keep, log_sig, -jnp.inf)
        stacked = jnp.concatenate([p[:, g * LANES:(g + 1) * LANES] for g in range(npages)], axis=0)
        sums = sum(_dot(part, ue_ref[...]) for part in _split2(stacked))
        car = car_ref[...]
        after = [None] * npages
        for g in range(npages - 1, -1, -1):
            page_sums = sums[g * rows:(g + 1) * rows]
            after[g] = page_sums[:, :LANES] + car
            car = car + page_sums[:, LANES:]
        car_ref[...] = car
        w = jnp.exp2(log_sig - jnp.concatenate(after, axis=1)).astype(BF16)
        for h in range(SB_HEADS):
            acc_ref[h] += _dot_nt(w[h * SUBLANES:(h + 1) * SUBLANES], vt_of(h).astype(BF16))

    step = lax.broadcasted_iota(jnp.int32, (rows, LANES), 0) % SUBLANES
    keep_new = lax.broadcasted_iota(jnp.int32, (rows, LANES), 1) < step
    block(lambda h: knew_ref[h], lambda h: vnew_ref[h], 1, keep_new)

    def chunk_body(c, carry):
        slot = c % 2

        @pl.when(c + 1 < chunks)
        def _():
            start(s, c + 1, 1 - slot)

        @pl.when(jnp.logical_and(c + 1 == chunks, s + 1 < nseq))
        def _():
            start(s + 1, 0, 1 - slot)

        wait(s, c, slot)
        block(lambda h: kbuf[slot, h], lambda h: vbuf[slot, h], group, None)
        return carry

    lax.fori_loop(0, chunks, chunk_body, 0)
    o_ref[...] = acc_ref[...]


def _paged_attn(page_table, bias, q, knew, vnew, ck, cv, layer):
    nseq, pages = page_table.shape
    page = ck.shape[-1]
    group = DEC_PAGES
    while pages % (2 * group):
        group //= 2
    row = lambda shp: pl.BlockSpec((None,) + shp, lambda i, *_: (i,) + (0,) * len(shp))
    grid_spec = pltpu.PrefetchScalarGridSpec(
        num_scalar_prefetch=1,
        grid=(nseq,),
        in_specs=[_smem_spec(), row(q.shape[1:]), row(knew.shape[1:]), row(vnew.shape[1:]),
                  pl.BlockSpec((page, 2 * page), lambda i, *_: (0, 0)),
                  pl.BlockSpec(memory_space=pl.ANY), pl.BlockSpec(memory_space=pl.ANY)],
        out_specs=row(q.shape[1:]),
        scratch_shapes=[pltpu.VMEM((2, SB_HEADS, SB_HEAD_DIM, group * page), F32),
                        pltpu.VMEM((2, SB_HEADS, SB_HEAD_DIM, group * page), F32),
                        pltpu.SemaphoreType.DMA((2, 2)),
                        pltpu.VMEM((SB_HEADS, SUBLANES, SB_HEAD_DIM), F32),
                        pltpu.VMEM((SB_HEADS * SUBLANES, LANES), F32)],
    )
    return pl.pallas_call(
        functools.partial(_paged_attn_kernel, layer=layer, pages=pages, group=group),
        out_shape=jax.ShapeDtypeStruct(q.shape, F32),
        grid_spec=grid_spec,
        compiler_params=_params("arbitrary"),
        name="paged_attn",
    )(page_table.reshape(-1), bias, q, knew, vnew, _suffix_and_total(page), ck, cv)


def _lower_bounds(lb_logits):
    p = jax.nn.softmax(lb_logits.astype(F32), axis=0)
    c = jnp.cumsum(p, axis=0)
    return c - c[0:1]


def kernel(x_prompt, x_sample, state_conv, state_hgrn, cache_k, cache_v, page_table, norm_ffn1, w_ffn1_gate, w_ffn1_up, w_ffn1_down, norm_mix, w_in, conv_w, hgrn_lb_logits, hgrn_gnorm, sb_bias, w_out, norm_ffn2, w_ffn2_gate, w_ffn2_up, w_ffn2_down, norm_final):
    depth = w_in.shape[0]
    bp, tp, d = x_prompt.shape
    bs, ts, _ = x_sample.shape
    heads_h = D_HGRN // HGRN_HEAD
    lower = _lower_bounds(hgrn_lb_logits)
    ck = jnp.transpose(cache_k, (0, 1, 3, 4, 2))
    cv = jnp.transpose(cache_v, (0, 1, 3, 4, 2))
    page = ck.shape[-1]
    bf = lambda w: w.astype(BF16)
    row = lambda v: v.reshape(1, -1)
    o_conv, o_hgrn, o_q = 0, 3 * D_CONV, 3 * D_CONV + 4 * D_HGRN
    o_k, o_v = o_q + D_SB, o_q + 2 * D_SB

    xp = x_prompt.reshape(bp * tp, d)
    xs = x_sample.reshape(bs * ts, d)
    conv_p, conv_s, hg_p, hg_s, kp_l, vp_l, ks_l, vs_l = [], [], [], [], [], [], [], []
    for l in range(depth):
        final = l == depth - 1
        w1 = (row(norm_ffn1[l]), bf(w_ffn1_gate[l]), bf(w_ffn1_up[l]), bf(w_ffn1_down[l]))
        w2 = (row(norm_ffn2[l]), bf(w_ffn2_gate[l]), bf(w_ffn2_up[l]), bf(w_ffn2_down[l]))
        wi = bf(w_in[l])
        wo = bf(w_out[l])
        gmix, gfin = row(norm_mix[l]), row(norm_final)
        lb, gn = row(lower[l]), row(hgrn_gnorm[l])

        xp = _ffn(xp, *w1)
        a_out, hh, q, kt, vt, ktb, vtb, cst = _proj(
            xp.reshape(bp, tp, d), gmix, wi[:, o_conv:o_hgrn], wi[:, o_hgrn:o_q], wi[:, o_q:o_k],
            wi[:, o_k:o_v].T, wi[:, o_v:].T, conv_w[l])
        b_out, sfin = _hgrn(hh, lb, gn, jnp.zeros((bp, heads_h, HGRN_HEAD, HGRN_HEAD), F32),
                            HGRN_CHUNK, HGRN_CHUNK)
        c_out = _attn(q, ktb, vtb, sb_bias[l])
        xp = _out_ffn(xp, a_out.reshape(bp * tp, -1), b_out.reshape(bp * tp, -1),
                      c_out.reshape(bp * tp, -1), wo, *w2, gfin, final)
        conv_p.append(cst)
        hg_p.append(jnp.swapaxes(sfin, -1, -2))
        kp_l.append(jnp.transpose(kt, (0, 3, 1, 2)))
        vp_l.append(jnp.transpose(vt, (0, 3, 1, 2)))

        xs = _ffn(xs, *w1)
        hs = _norm_matmul(xs, gmix, wi)
        tm = lambda c0, w: jnp.swapaxes(hs[:, c0:c0 + w].reshape(bs, ts, w), 0, 1)
        a_s, cs_new = _sample_conv(tm(0, D_CONV), tm(D_CONV, D_CONV), tm(2 * D_CONV, D_CONV),
                                   jnp.swapaxes(state_conv[l], 0, 1), conv_w[l])
        hh_s = jnp.pad(hs[:, o_hgrn:o_q].reshape(bs, ts, -1), ((0, 0), (0, SUBLANES - ts), (0, 0)))
        b_s, sfin_s = _hgrn(hh_s, lb, gn, jnp.swapaxes(state_hgrn[l], -1, -2), SUBLANES, ts)
        heads4 = lambda c0: hs[:, c0:c0 + D_SB].reshape(bs, ts, SB_HEADS, SB_HEAD_DIM)
        q_s = jnp.pad(jnp.swapaxes(heads4(o_q), 1, 2), ((0, 0), (0, 0), (0, SUBLANES - ts), (0, 0)))
        k_s, v_s = heads4(o_k), heads4(o_v)
        new_t = lambda a: jnp.pad(jnp.transpose(a, (0, 2, 3, 1)),
                                  ((0, 0), (0, 0), (0, 0), (0, page - ts)))
        c_s = _paged_attn(page_table, sb_bias[l], q_s, new_t(k_s), new_t(v_s), ck, cv, l)
        c_s = jnp.swapaxes(c_s[:, :, :ts, :], 1, 2).reshape(bs * ts, D_SB).astype(BF16)
        xs = _out_ffn(xs, jnp.swapaxes(a_s, 0, 1).reshape(bs * ts, -1),
                      b_s[:, :ts, :].reshape(bs * ts, -1), c_s, wo, *w2, gfin, final)
        conv_s.append(jnp.swapaxes(cs_new, 0, 1))
        hg_s.append(jnp.swapaxes(sfin_s, -1, -2))
        ks_l.append(k_s)
        vs_l.append(v_s)

    return (xp.reshape(bp, tp, d), xs.reshape(bs, ts, d), jnp.stack(conv_p), jnp.stack(conv_s),
            jnp.stack(hg_p), jnp.stack(hg_s), jnp.stack(kp_l), jnp.stack(vp_l),
            jnp.stack(ks_l), jnp.stack(vs_l))
```

```python
import functools
import math

import jax
import jax.numpy as jnp
from jax import lax
from jax.experimental import pallas as pl
from jax.experimental.pallas import tpu as pltpu

F32 = jnp.float32
BF16 = jnp.bfloat16

RMS_EPS = 1e-6
D_CONV = 256
D_HGRN = 256
HGRN_HEAD = 64
D_SB = 512
SB_HEADS = 8
SB_HEAD_DIM = 64
SB_SCALE = SB_HEAD_DIM ** -0.5
LOG2E = math.log2(math.e)

V7X_VMEM_BYTES = 64 * 1024 * 1024
VMEM_LIMIT = V7X_VMEM_BYTES - 8 * 1024 * 1024
LANES = 128
SUBLANES = 8

FFN_ROWS = 256
HGRN_CHUNK = 64
HGRN_GROUP = 2
ATT_Q = 256
DEC_PAGES = 16


def _const_spec(shape):
    nd = len(shape)
    return pl.BlockSpec(shape, lambda *_: (0,) * nd, pipeline_mode=pl.Buffered(1))


def _smem_spec():
    return pl.BlockSpec(memory_space=pltpu.SMEM)


def _params(*sem):
    return pltpu.CompilerParams(dimension_semantics=sem, vmem_limit_bytes=VMEM_LIMIT)


def _rms(x, g):
    return x * lax.rsqrt(jnp.mean(x * x, axis=-1, keepdims=True) + RMS_EPS) * g


def _split2(x):
    hi = x.astype(BF16)
    lo = (x - hi.astype(F32)).astype(BF16)
    return hi, lo


def _split3(x):
    hi = x.astype(BF16)
    r = x - hi.astype(F32)
    mid = r.astype(BF16)
    lo = (r - mid.astype(F32)).astype(BF16)
    return hi, mid, lo


def _dot(a, b):
    return jnp.dot(a, b, preferred_element_type=F32)


def _dot_nt(a, b):
    return lax.dot_general(a, b, (((1,), (1,)), ((), ())), preferred_element_type=F32)


def _dot_tn(a, b):
    return lax.dot_general(a, b, (((0,), (0,)), ((), ())), preferred_element_type=F32)


def _log_not_sigmoid(z):
    nz = -z
    l1p = jnp.log1p(jnp.exp(jnp.minimum(z, nz)))
    log_not = jnp.minimum(nz, 0.0) - l1p
    return log_not, z + log_not


def _swiglu_residual(x, g, wg_ref, wu_ref, wd_ref):
    n = _rms(x, g).astype(BF16)
    gate = _dot(n, wg_ref[...])
    up = _dot(n, wu_ref[...])
    act = (gate * jax.nn.sigmoid(gate) * up).astype(BF16)
    return x + 0.5 * _dot(act, wd_ref[...])


def _ffn_kernel(x_ref, g_ref, wg_ref, wu_ref, wd_ref, o_ref):
    o_ref[...] = _swiglu_residual(x_ref[...], g_ref[...], wg_ref, wu_ref, wd_ref)


def _ffn(x, g, wg, wu, wd):
    n, d = x.shape
    rows = min(FFN_ROWS, n)
    tile = pl.BlockSpec((rows, d), lambda i: (i, 0))
    return pl.pallas_call(
        _ffn_kernel,
        out_shape=jax.ShapeDtypeStruct((n, d), F32),
        grid=(n // rows,),
        in_specs=[tile, _const_spec((1, d)), _const_spec(wg.shape), _const_spec(wu.shape),
                  _const_spec(wd.shape)],
        out_specs=tile,
        compiler_params=_params("parallel"),
        name="ffn",
    )(x, g, wg, wu, wd)


def _out_ffn_kernel(x_ref, a_ref, b_ref, c_ref, wo_ref, g_ref, wg_ref, wu_ref, wd_ref, gf_ref,
                    o_ref, *, final):
    mix = jnp.concatenate([a_ref[...], b_ref[...], c_ref[...]], axis=-1)
    x = x_ref[...] + _dot(mix, wo_ref[...])
    x = _swiglu_residual(x, g_ref[...], wg_ref, wu_ref, wd_ref)
    o_ref[...] = _rms(x, gf_ref[...]) if final else x


def _out_ffn(x, a, b, c, wo, g, wg, wu, wd, gf, final):
    n, d = x.shape
    rows = min(FFN_ROWS, n)

    def tile(w):
        return pl.BlockSpec((rows, w), lambda i: (i, 0))

    return pl.pallas_call(
        functools.partial(_out_ffn_kernel, final=final),
        out_shape=jax.ShapeDtypeStruct((n, d), F32),
        grid=(n // rows,),
        in_specs=[tile(d), tile(a.shape[1]), tile(b.shape[1]), tile(c.shape[1]),
                  _const_spec(wo.shape), _const_spec((1, d)), _const_spec(wg.shape),
                  _const_spec(wu.shape), _const_spec(wd.shape), _const_spec((1, d))],
        out_specs=tile(d),
        compiler_params=_params("parallel"),
        name="out_ffn",
    )(x, a, b, c, wo, g, wg, wu, wd, gf)


def _norm_matmul_kernel(x_ref, g_ref, w_ref, o_ref):
    o_ref[...] = _dot(_rms(x_ref[...], g_ref[...]).astype(BF16), w_ref[...])


def _norm_matmul(x, g, w):
    n, d = x.shape
    return pl.pallas_call(
        _norm_matmul_kernel,
        out_shape=jax.ShapeDtypeStruct((n, w.shape[1]), F32),
        grid=(1,),
        in_specs=[_const_spec((n, d)), _const_spec((1, d)), _const_spec(w.shape)],
        out_specs=pl.BlockSpec((n, w.shape[1]), lambda i: (0, 0)),
        compiler_params=_params("arbitrary"),
        name="norm_matmul",
    )(x, g, w)


def _proj_kernel(x_ref, g_ref, wc_ref, wh_ref, wq_ref, wkt_ref, wvt_ref, cw_ref,
                 a_ref, hh_ref, q_ref, kt_ref, vt_ref, ktb_ref, vtb_ref, cs_ref, carry_ref):
    @pl.when(pl.program_id(1) == 0)
    def _():
        carry_ref[...] = jnp.zeros_like(carry_ref)

    n = _rms(x_ref[...], g_ref[...]).astype(BF16)
    rows = n.shape[0]

    hc = _dot(n, wc_ref[...])
    bg, cg, xa = hc[:, :D_CONV], hc[:, D_CONV:2 * D_CONV], hc[:, 2 * D_CONV:]
    u = cg * xa
    prev = carry_ref[...]
    p1, p2 = prev[SUBLANES - 1:SUBLANES, :], prev[SUBLANES - 2:SUBLANES - 1, :]
    row = lax.broadcasted_iota(jnp.int32, u.shape, 0)
    u1 = jnp.where(row >= 1, pltpu.roll(u, 1, 0), p1)
    u2 = jnp.where(row >= 2, pltpu.roll(u, 2, 0), jnp.where(row == 1, p1, p2))
    cw = cw_ref[...]
    y = cw[0:1, :] * u2 + cw[1:2, :] * u1 + cw[2:3, :] * u
    a_ref[...] = (bg * y).astype(BF16)
    carry_ref[...] = u[rows - SUBLANES:, :]
    cs_ref[...] = u[rows - 2:, :]

    hh_ref[...] = _dot(n, wh_ref[...])
    q_ref[...] = (_dot(n, wq_ref[...]) * (SB_SCALE * LOG2E)).astype(BF16)
    kt = _dot_nt(wkt_ref[...], n).reshape(SB_HEADS, SB_HEAD_DIM, rows)
    vt = _dot_nt(wvt_ref[...], n).reshape(SB_HEADS, SB_HEAD_DIM, rows)
    kt_ref[...] = kt
    vt_ref[...] = vt
    ktb_ref[...] = kt.astype(BF16)
    vtb_ref[...] = vt.astype(BF16)


def _proj(x, g, wc, wh, wq, wkt, wvt, cw):
    b, t, d = x.shape
    rows = FFN_ROWS

    def tile(w):
        return pl.BlockSpec((None, rows, w), lambda i, j: (i, j, 0))

    kv_tile = pl.BlockSpec((None, SB_HEADS, SB_HEAD_DIM, rows), lambda i, j: (i, 0, 0, j))
    kv_shape = (b, SB_HEADS, SB_HEAD_DIM, t)
    return pl.pallas_call(
        _proj_kernel,
        out_shape=(jax.ShapeDtypeStruct((b, t, D_CONV), BF16),
                   jax.ShapeDtypeStruct((b, t, 4 * D_HGRN), F32),
                   jax.ShapeDtypeStruct((b, t, D_SB), BF16),
                   jax.ShapeDtypeStruct(kv_shape, F32), jax.ShapeDtypeStruct(kv_shape, F32),
                   jax.ShapeDtypeStruct(kv_shape, BF16), jax.ShapeDtypeStruct(kv_shape, BF16),
                   jax.ShapeDtypeStruct((b, 2, D_CONV), F32)),
        grid=(b, t // rows),
        in_specs=[tile(d), _const_spec((1, d)), _const_spec(wc.shape), _const_spec(wh.shape),
                  _const_spec(wq.shape), _const_spec(wkt.shape), _const_spec(wvt.shape),
                  _const_spec(cw.shape)],
        out_specs=(tile(D_CONV), tile(4 * D_HGRN), tile(D_SB), kv_tile, kv_tile, kv_tile, kv_tile,
                   pl.BlockSpec((None, 2, D_CONV), lambda i, j: (i, 0, 0))),
        scratch_shapes=[pltpu.VMEM((SUBLANES, D_CONV), F32)],
        compiler_params=_params("arbitrary", "arbitrary"),
        name="proj",
    )(x, g, wc, wh, wq, wkt, wvt, cw)


def _sample_conv_kernel(bg_ref, cg_ref, xa_ref, prev_ref, cw_ref, a_ref, cs_ref):
    steps = bg_ref.shape[0]
    up = [prev_ref[0], prev_ref[1]] + [cg_ref[t] * xa_ref[t] for t in range(steps)]
    cw = cw_ref[...]
    for t in range(steps):
        y = cw[0:1, :] * up[t] + cw[1:2, :] * up[t + 1] + cw[2:3, :] * up[t + 2]
        a_ref[t] = (bg_ref[t] * y).astype(BF16)
    cs_ref[0] = up[steps]
    cs_ref[1] = up[steps + 1]


def _sample_conv(bg, cg, xa, prev, cw):
    full = lambda s: pl.BlockSpec(s, lambda i: (0,) * len(s))
    return pl.pallas_call(
        _sample_conv_kernel,
        out_shape=(jax.ShapeDtypeStruct(bg.shape, BF16), jax.ShapeDtypeStruct(prev.shape, F32)),
        grid=(1,),
        in_specs=[full(bg.shape), full(cg.shape), full(xa.shape), full(prev.shape), full(cw.shape)],
        out_specs=(full(bg.shape), full(prev.shape)),
        compiler_params=_params("arbitrary"),
        name="sample_conv",
    )(bg, cg, xa, prev, cw)


def _hgrn_kernel(hh_ref, lb_ref, gn_ref, s0_ref, tril_ref, ones_ref, o_ref, sfin_ref,
                 st_ref, b_ref, d_ref, ae_ref, oi_ref, *, chunk, valid):
    group = hh_ref.shape[0]
    head_slices = [slice(h * HGRN_HEAD, (h + 1) * HGRN_HEAD) for h in range(D_HGRN // HGRN_HEAD)]

    @pl.when(pl.program_id(1) == 0)
    def _():
        st_ref[...] = jnp.zeros_like(st_ref)
        for s in range(group):
            for h, sl in enumerate(head_slices):
                st_ref[s, sl, sl] = s0_ref[s, h]

    for s in range(group):
        _hgrn_chunk(hh_ref.at[s], lb_ref, gn_ref, tril_ref, ones_ref, o_ref.at[s], st_ref.at[s],
                    b_ref.at[s], d_ref.at[s], ae_ref.at[s], oi_ref.at[s], chunk=chunk, valid=valid)

    @pl.when(pl.program_id(1) == pl.num_programs(1) - 1)
    def _():
        for s in range(group):
            for h, sl in enumerate(head_slices):
                sfin_ref[s, h] = st_ref[s, sl, sl]


def _hgrn_chunk(hh_ref, lb_ref, gn_ref, tril_ref, ones_ref, o_ref, st_ref, b_ref, d_ref, ae_ref,
                oi_ref, *, chunk, valid):
    q = hh_ref[:, 0:D_HGRN]
    z = hh_ref[:, D_HGRN:2 * D_HGRN]
    v = hh_ref[:, 2 * D_HGRN:3 * D_HGRN]
    g = hh_ref[:, 3 * D_HGRN:4 * D_HGRN]
    lb = lb_ref[...]

    _, log_sig = _log_not_sigmoid(z)
    la = jnp.log(lb)
    lc = jnp.log1p(-lb) + log_sig
    logf = jnp.maximum(la, lc) + jnp.log1p(jnp.exp(-jnp.abs(la - lc)))
    k = (1.0 - lb) * jax.nn.sigmoid(-z)
    srow = lax.broadcasted_iota(jnp.int32, (chunk, 1), 0)
    if valid < chunk:
        logf = jnp.where(srow < valid, logf, 0.0)
        k = jnp.where(srow < valid, k, 0.0)

    tril = tril_ref[...]
    b = sum(_dot(tril, part) for part in _split3(logf))
    b_last = b[chunk - 1:chunk, :]
    b_ref[...] = b

    st = st_ref[...]
    o_inter = _dot_nt((q * jnp.exp(b)).astype(BF16), st.astype(BF16))

    tiles = chunk // SUBLANES
    half_tile = SUBLANES * SUBLANES // 2

    def pair_rows(i, r):
        n = SUBLANES * (i + 1)
        start = half_tile * i * (i + 1) + r * n
        return pl.ds(start if isinstance(start, int) else pl.multiple_of(start, SUBLANES), n)

    for i in range(tiles):
        n = SUBLANES * (i + 1)

        def build(r, carry, i=i, n=n):
            t = i * SUBLANES + r
            bt = b_ref[pl.ds(t, 1), :]
            qt = hh_ref[pl.ds(t, 1), 0:D_HGRN]
            dec = jnp.exp(jnp.where(srow[:n] <= t, bt - b[:n], -jnp.inf))
            d_ref[pair_rows(i, r), :] = (qt * k[:n]) * dec
            return carry

        lax.fori_loop(0, SUBLANES, build, 0, unroll=True)
    ae_ref[...] = _dot(d_ref[...].astype(BF16), ones_ref[...])

    for i in range(tiles):
        n = SUBLANES * (i + 1)

        def apply(r, carry, i=i, n=n):
            oi_ref[pl.ds(i * SUBLANES + r, 1), :] = jnp.sum(ae_ref[pair_rows(i, r), :] * v[:n], axis=0,
                                                           keepdims=True)
            return carry

        lax.fori_loop(0, SUBLANES, apply, 0, unroll=True)
    o = o_inter + oi_ref[...]

    head_mask = ones_ref[...].astype(F32)
    kd = k * jnp.exp(b_last - b)
    st_ref[...] = st * jnp.exp(b_last) + _dot_tn(v.astype(BF16), kd.astype(BF16)) * head_mask

    ms = sum(_dot(part, ones_ref[...]) for part in _split2(o * o)) * (1.0 / HGRN_HEAD)
    y = o * lax.rsqrt(ms + RMS_EPS) * gn_ref[...]
    o_ref[...] = (y * (g * jax.nn.sigmoid(g))).astype(BF16)


def _hgrn(hh, lb, gn, s0t, chunk, valid):
    bsz, t, _ = hh.shape
    heads = D_HGRN // HGRN_HEAD
    r = jnp.arange(chunk)
    tril = (r[:, None] >= r[None, :]).astype(BF16)
    hd = jnp.arange(D_HGRN) // HGRN_HEAD
    ones = (hd[:, None] == hd[None, :]).astype(BF16)
    group = HGRN_GROUP
    st_spec = pl.BlockSpec((group, heads, HGRN_HEAD, HGRN_HEAD), lambda i, j: (i, 0, 0, 0))
    tiles = chunk // SUBLANES
    pairs = (SUBLANES * SUBLANES // 2) * tiles * (tiles + 1)
    return pl.pallas_call(
        functools.partial(_hgrn_kernel, chunk=chunk, valid=valid),
        out_shape=(jax.ShapeDtypeStruct((bsz, t, D_HGRN), BF16),
                   jax.ShapeDtypeStruct(s0t.shape, F32)),
        grid=(bsz // group, t // chunk),
        in_specs=[pl.BlockSpec((group, chunk, 4 * D_HGRN), lambda i, j: (i, j, 0)),
                  _const_spec((1, D_HGRN)), _const_spec((1, D_HGRN)), st_spec,
                  _const_spec(tril.shape), _const_spec(ones.shape)],
        out_specs=(pl.BlockSpec((group, chunk, D_HGRN), lambda i, j: (i, j, 0)), st_spec),
        scratch_shapes=[pltpu.VMEM((group, D_HGRN, D_HGRN), F32),
                        pltpu.VMEM((group, chunk, D_HGRN), F32),
                        pltpu.VMEM((group, pairs, D_HGRN), F32),
                        pltpu.VMEM((group, pairs, D_HGRN), F32),
                        pltpu.VMEM((group, chunk, D_HGRN), F32)],
        compiler_params=_params("arbitrary", "arbitrary"),
        name="hgrn",
    )(hh, lb, gn, s0t, tril, ones)


def _suffix_ones(n):
    r = jnp.arange(n)
    return (r[:, None] > r[None, :]).astype(BF16)


def _suffix_and_total(n):
    return jnp.concatenate([_suffix_ones(n), jnp.ones((n, n), BF16)], axis=1)


def _neg_log2_not_sigmoid(z):
    pos = jnp.maximum(z, 0.0)
    neg = z - pos
    soft = jnp.log(1.0 + jnp.exp2(neg - pos)) * LOG2E
    return pos + soft, neg - soft


def _attn_kernel(bias_ref, q_ref, kt_ref, vt_ref, u_ref, o_ref, acc_ref, car_ref, ls_ref, r_ref,
                 *, tq, tk):
    i = pl.program_id(1)
    heads = q_ref.shape[1] // SB_HEAD_DIM
    bias = [bias_ref[h] * LOG2E for h in range(heads)]
    lane_tiles = tk // LANES

    def col(j):
        return pl.ds(pl.multiple_of(j * tk, tk), tk)

    def scores(h, j):
        return _dot(q_ref[:, h * SB_HEAD_DIM:(h + 1) * SB_HEAD_DIM], kt_ref[h, :, col(j)])

    def logs(h, z, keep):
        p, log_sig = _neg_log2_not_sigmoid(z + bias[h])
        if keep is not None:
            p = jnp.where(keep, p, 0.0)
            log_sig = jnp.where(keep, log_sig, -jnp.inf)
        ls_ref[h] = log_sig
        r_ref[h] = _dot(p.astype(BF16), u_ref[...])
        return jnp.broadcast_to(jnp.sum(p, axis=1, keepdims=True), (tq, LANES))

    def weights(h, j):
        after = r_ref[h] + jnp.concatenate([car_ref[h]] * lane_tiles, axis=1)
        w = jnp.exp2(ls_ref[h] - after)
        acc_ref[h] += _dot_nt(w.astype(BF16), vt_ref[h, :, col(j)])

    keep = (lax.broadcasted_iota(jnp.int32, (tq, tk), 1) < lax.broadcasted_iota(jnp.int32, (tq, tk), 0))
    for h in range(heads):
        car_ref[h] = jnp.zeros((tq, LANES), F32)
        car_ref[heads + h] = logs(h, scores(h, i), keep)
        acc_ref[h] = jnp.zeros((tq, SB_HEAD_DIM), F32)

    def body(jj, carry):
        j = i - 1 - jj
        zs = [scores(h, j) for h in range(heads)]
        for h in range(heads):
            weights(h, j + 1)
        for h in range(heads):
            total = logs(h, zs[h], None)
            c = car_ref[heads + h]
            car_ref[h] = c
            car_ref[heads + h] = c + total
        return carry

    lax.fori_loop(0, i, body, 0)
    for h in range(heads):
        weights(h, 0)
    o_ref[...] = jnp.concatenate([acc_ref[h] for h in range(heads)], axis=-1).astype(BF16)


def _attn(q, ktb, vtb, bias):
    b, t, _ = q.shape
    tq = tk = min(ATT_Q, t)
    kv_spec = pl.BlockSpec((None, SB_HEADS, SB_HEAD_DIM, t), lambda bi, i: (bi, 0, 0, 0),
                           pipeline_mode=pl.Buffered(1))
    qo_spec = pl.BlockSpec((None, tq, D_SB), lambda bi, i: (bi, i, 0))
    return pl.pallas_call(
        functools.partial(_attn_kernel, tq=tq, tk=tk),
        out_shape=jax.ShapeDtypeStruct(q.shape, BF16),
        grid=(b, t // tq),
        in_specs=[_smem_spec(), qo_spec, kv_spec, kv_spec, _const_spec((tk, tk))],
        out_specs=qo_spec,
        scratch_shapes=[pltpu.VMEM((SB_HEADS, tq, SB_HEAD_DIM), F32),
                        pltpu.VMEM((2 * SB_HEADS, tq, LANES), F32),
                        pltpu.VMEM((SB_HEADS, tq, tk), F32),
                        pltpu.VMEM((SB_HEADS, tq, tk), F32)],
        compiler_params=_params("parallel", "arbitrary"),
        name="attn",
    )(bias, q, ktb, vtb, _suffix_ones(tk))


def _paged_attn_kernel(pt_ref, bias_ref, q_ref, knew_ref, vnew_ref, ue_ref, ck_ref, cv_ref, o_ref,
                       kbuf, vbuf, sem, acc_ref, car_ref, *, layer, pages, group):
    s = pl.program_id(0)
    nseq = pl.num_programs(0)
    chunks = pages // group
    rows = SB_HEADS * SUBLANES

    def copies(seq, c, slot):
        out = []
        for g in range(group):
            page = pt_ref[seq * pages + (chunks - 1 - c) * group + g]
            lanes = pl.ds(g * LANES, LANES)
            out.append(pltpu.make_async_copy(ck_ref.at[layer, page], kbuf.at[slot, :, :, lanes],
                                             sem.at[0, slot]))
            out.append(pltpu.make_async_copy(cv_ref.at[layer, page], vbuf.at[slot, :, :, lanes],
                                             sem.at[1, slot]))
        return out

    def start(seq, c, slot):
        for cp in copies(seq, c, slot):
            cp.start()

    def wait(seq, c, slot):
        for cp in copies(seq, c, slot):
            cp.wait()

    @pl.when(s == 0)
    def _():
        start(0, 0, 0)

    acc_ref[...] = jnp.zeros_like(acc_ref)
    car_ref[...] = jnp.zeros_like(car_ref)
    q = (q_ref[...] * (SB_SCALE * LOG2E)).astype(BF16)
    bias = [bias_ref[h] * LOG2E for h in range(SB_HEADS)]

    def block(kt_of, vt_of, npages, keep):
        z = jnp.concatenate(
            [_dot(q[h], kt_of(h).astype(BF16)) + bias[h] for h in range(SB_HEADS)], axis=0)
        p, log_sig = _neg_log2_not_sigmoid(z)
        if keep is not None:
            p = jnp.where(keep, p, 0.0)
            log_sig = jnp.where(keep, log_sig, -jnp.inf)
        stacked = jnp.concatenate([p[:, g * LANES:(g + 1) * LANES] for g in range(npages)], axis=0)
        sums = _dot(stacked.astype(BF16), ue_ref[...])
        car = car_ref[...]
        after = [None] * npages
        for g in range(npages - 1, -1, -1):
            page_sums = sums[g * rows:(g + 1) * rows]
            after[g] = page_sums[:, :LANES] + car
            car = car + page_sums[:, LANES:]
        car_ref[...] = car
        w = jnp.exp2(log_sig - jnp.concatenate(after, axis=1)).astype(BF16)
        for h in range(SB_HEADS):
            acc_ref[h] += _dot_nt(w[h * SUBLANES:(h + 1) * SUBLANES], vt_of(h).astype(BF16))

    step = lax.broadcasted_iota(jnp.int32, (rows, LANES), 0) % SUBLANES
    keep_new = lax.broadcasted_iota(jnp.int32, (rows, LANES), 1) < step
    block(lambda h: knew_ref[h], lambda h: vnew_ref[h], 1, keep_new)

    def chunk_body(c, carry):
        slot = c % 2

        @pl.when(c + 1 < chunks)
        def _():
            start(s, c + 1, 1 - slot)

        @pl.when(jnp.logical_and(c + 1 == chunks, s + 1 < nseq))
        def _():
            start(s + 1, 0, 1 - slot)

        wait(s, c, slot)
        block(lambda h: kbuf[slot, h], lambda h: vbuf[slot, h], group, None)
        return carry

    lax.fori_loop(0, chunks, chunk_body, 0)
    o_ref[...] = acc_ref[...]


def _paged_attn(page_table, bias, q, knew, vnew, ck, cv, layer):
    nseq, pages = page_table.shape
    page = ck.shape[-1]
    group = DEC_PAGES
    while pages % (2 * group):
        group //= 2
    row = lambda shp: pl.BlockSpec((None,) + shp, lambda i, *_: (i,) + (0,) * len(shp))
    grid_spec = pltpu.PrefetchScalarGridSpec(
        num_scalar_prefetch=1,
        grid=(nseq,),
        in_specs=[_smem_spec(), row(q.shape[1:]), row(knew.shape[1:]), row(vnew.shape[1:]),
                  pl.BlockSpec((page, 2 * page), lambda i, *_: (0, 0)),
                  pl.BlockSpec(memory_space=pl.ANY), pl.BlockSpec(memory_space=pl.ANY)],
        out_specs=row(q.shape[1:]),
        scratch_shapes=[pltpu.VMEM((2, SB_HEADS, SB_HEAD_DIM, group * page), F32),
                        pltpu.VMEM((2, SB_HEADS, SB_HEAD_DIM, group * page), F32),
                        pltpu.SemaphoreType.DMA((2, 2)),
                        pltpu.VMEM((SB_HEADS, SUBLANES, SB_HEAD_DIM), F32),
                        pltpu.VMEM((SB_HEADS * SUBLANES, LANES), F32)],
    )
    return pl.pallas_call(
        functools.partial(_paged_attn_kernel, layer=layer, pages=pages, group=group),
        out_shape=jax.ShapeDtypeStruct(q.shape, F32),
        grid_spec=grid_spec,
        compiler_params=_params("arbitrary"),
        name="paged_attn",
    )(page_table.reshape(-1), bias, q, knew, vnew, _suffix_and_total(page), ck, cv)


def _lower_bounds(lb_logits):
    p = jax.nn.softmax(lb_logits.astype(F32), axis=0)
    c = jnp.cumsum(p, axis=0)
    return c - c[0:1]


def kernel(x_prompt, x_sample, state_conv, state_hgrn, cache_k, cache_v, page_table, norm_ffn1, w_ffn1_gate, w_ffn1_up, w_ffn1_down, norm_mix, w_in, conv_w, hgrn_lb_logits, hgrn_gnorm, sb_bias, w_out, norm_ffn2, w_ffn2_gate, w_ffn2_up, w_ffn2_down, norm_final):
    depth = w_in.shape[0]
    bp, tp, d = x_prompt.shape
    bs, ts, _ = x_sample.shape
    heads_h = D_HGRN // HGRN_HEAD
    lower = _lower_bounds(hgrn_lb_logits)
    ck = jnp.transpose(cache_k, (0, 1, 3, 4, 2))
    cv = jnp.transpose(cache_v, (0, 1, 3, 4, 2))
    page = ck.shape[-1]
    bf = lambda w: w.astype(BF16)
    row = lambda v: v.reshape(1, -1)
    o_conv, o_hgrn, o_q = 0, 3 * D_CONV, 3 * D_CONV + 4 * D_HGRN
    o_k, o_v = o_q + D_SB, o_q + 2 * D_SB

    xp = x_prompt.reshape(bp * tp, d)
    xs = x_sample.reshape(bs * ts, d)
    conv_p, conv_s, hg_p, hg_s, kp_l, vp_l, ks_l, vs_l = [], [], [], [], [], [], [], []
    for l in range(depth):
        final = l == depth - 1
        w1 = (row(norm_ffn1[l]), bf(w_ffn1_gate[l]), bf(w_ffn1_up[l]), bf(w_ffn1_down[l]))
        w2 = (row(norm_ffn2[l]), bf(w_ffn2_gate[l]), bf(w_ffn2_up[l]), bf(w_ffn2_down[l]))
        wi = bf(w_in[l])
        wo = bf(w_out[l])
        gmix, gfin = row(norm_mix[l]), row(norm_final)
        lb, gn = row(lower[l]), row(hgrn_gnorm[l])

        xp = _ffn(xp, *w1)
        a_out, hh, q, kt, vt, ktb, vtb, cst = _proj(
            xp.reshape(bp, tp, d), gmix, wi[:, o_conv:o_hgrn], wi[:, o_hgrn:o_q], wi[:, o_q:o_k],
            wi[:, o_k:o_v].T, wi[:, o_v:].T, conv_w[l])
        b_out, sfin = _hgrn(hh, lb, gn, jnp.zeros((bp, heads_h, HGRN_HEAD, HGRN_HEAD), F32),
                            HGRN_CHUNK, HGRN_CHUNK)
        c_out = _attn(q, ktb, vtb, sb_bias[l])
        xp = _out_ffn(xp, a_out.reshape(bp * tp, -1), b_out.reshape(bp * tp, -1),
                      c_out.reshape(bp * tp, -1), wo, *w2, gfin, final)
        conv_p.append(cst)
        hg_p.append(jnp.swapaxes(sfin, -1, -2))
        kp_l.append(jnp.transpose(kt, (0, 3, 1, 2)))
        vp_l.append(jnp.transpose(vt, (0, 3, 1, 2)))

        xs = _ffn(xs, *w1)
        hs = _norm_matmul(xs, gmix, wi)
        tm = lambda c0, w: jnp.swapaxes(hs[:, c0:c0 + w].reshape(bs, ts, w), 0, 1)
        a_s, cs_new = _sample_conv(tm(0, D_CONV), tm(D_CONV, D_CONV), tm(2 * D_CONV, D_CONV),
                                   jnp.swapaxes(state_conv[l], 0, 1), conv_w[l])
        hh_s = jnp.pad(hs[:, o_hgrn:o_q].reshape(bs, ts, -1), ((0, 0), (0, SUBLANES - ts), (0, 0)))
        b_s, sfin_s = _hgrn(hh_s, lb, gn, jnp.swapaxes(state_hgrn[l], -1, -2), SUBLANES, ts)
        heads4 = lambda c0: hs[:, c0:c0 + D_SB].reshape(bs, ts, SB_HEADS, SB_HEAD_DIM)
        q_s = jnp.pad(jnp.swapaxes(heads4(o_q), 1, 2), ((0, 0), (0, 0), (0, SUBLANES - ts), (0, 0)))
        k_s, v_s = heads4(o_k), heads4(o_v)
        new_t = lambda a: jnp.pad(jnp.transpose(a, (0, 2, 3, 1)),
                                  ((0, 0), (0, 0), (0, 0), (0, page - ts)))
        c_s = _paged_attn(page_table, sb_bias[l], q_s, new_t(k_s), new_t(v_s), ck, cv, l)
        c_s = jnp.swapaxes(c_s[:, :, :ts, :], 1, 2).reshape(bs * ts, D_SB).astype(BF16)
        xs = _out_ffn(xs, jnp.swapaxes(a_s, 0, 1).reshape(bs * ts, -1),
                      b_s[:, :ts, :].reshape(bs * ts, -1), c_s, wo, *w2, gfin, final)
        conv_s.append(jnp.swapaxes(cs_new, 0, 1))
        hg_s.append(jnp.swapaxes(sfin_s, -1, -2))
        ks_l.append(k_s)
        vs_l.append(v_s)

    return (xp.reshape(bp, tp, d), xs.reshape(bs, ts, d), jnp.stack(conv_p), jnp.stack(conv_s),
            jnp.stack(hg_p), jnp.stack(hg_s), jnp.stack(kp_l), jnp.stack(vp_l),
            jnp.stack(ks_l), jnp.stack(vs_l))
```

```python
import functools
import math

import jax
import jax.numpy as jnp
from jax import lax
from jax.experimental import pallas as pl
from jax.experimental.pallas import tpu as pltpu

F32 = jnp.float32
BF16 = jnp.bfloat16

RMS_EPS = 1e-6
D_CONV = 256
D_HGRN = 256
HGRN_HEAD = 64
D_SB = 512
SB_HEADS = 8
SB_HEAD_DIM = 64
SB_SCALE = SB_HEAD_DIM ** -0.5
LOG2E = math.log2(math.e)

V7X_VMEM_BYTES = 64 * 1024 * 1024
VMEM_LIMIT = V7X_VMEM_BYTES - 8 * 1024 * 1024
LANES = 128
SUBLANES = 8

FFN_ROWS = 256
HGRN_CHUNK = 64
HGRN_GROUP = 2
HGRN_SUB = 16
ATT_Q = 256
DEC_PAGES = 16


def _const_spec(shape):
    nd = len(shape)
    return pl.BlockSpec(shape, lambda *_: (0,) * nd, pipeline_mode=pl.Buffered(1))


def _smem_spec():
    return pl.BlockSpec(memory_space=pltpu.SMEM)


def _params(*sem):
    return pltpu.CompilerParams(dimension_semantics=sem, vmem_limit_bytes=VMEM_LIMIT)


def _rms(x, g):
    return x * lax.rsqrt(jnp.mean(x * x, axis=-1, keepdims=True) + RMS_EPS) * g


def _split2(x):
    hi = x.astype(BF16)
    lo = (x - hi.astype(F32)).astype(BF16)
    return hi, lo


def _split3(x):
    hi = x.astype(BF16)
    r = x - hi.astype(F32)
    mid = r.astype(BF16)
    lo = (r - mid.astype(F32)).astype(BF16)
    return hi, mid, lo


def _dot(a, b):
    return jnp.dot(a, b, preferred_element_type=F32)


def _dot_nt(a, b):
    return lax.dot_general(a, b, (((1,), (1,)), ((), ())), preferred_element_type=F32)


def _dot_tn(a, b):
    return lax.dot_general(a, b, (((0,), (0,)), ((), ())), preferred_element_type=F32)


def _log_not_sigmoid(z):
    nz = -z
    l1p = jnp.log1p(jnp.exp(jnp.minimum(z, nz)))
    log_not = jnp.minimum(nz, 0.0) - l1p
    return log_not, z + log_not


def _swiglu_residual(x, g, wg_ref, wu_ref, wd_ref):
    n = _rms(x, g).astype(BF16)
    gate = _dot(n, wg_ref[...])
    up = _dot(n, wu_ref[...])
    act = (gate * jax.nn.sigmoid(gate) * up).astype(BF16)
    return x + 0.5 * _dot(act, wd_ref[...])


def _ffn_kernel(x_ref, g_ref, wg_ref, wu_ref, wd_ref, o_ref):
    o_ref[...] = _swiglu_residual(x_ref[...], g_ref[...], wg_ref, wu_ref, wd_ref)


def _ffn(x, g, wg, wu, wd):
    n, d = x.shape
    rows = min(FFN_ROWS, n)
    tile = pl.BlockSpec((rows, d), lambda i: (i, 0))
    return pl.pallas_call(
        _ffn_kernel,
        out_shape=jax.ShapeDtypeStruct((n, d), F32),
        grid=(n // rows,),
        in_specs=[tile, _const_spec((1, d)), _const_spec(wg.shape), _const_spec(wu.shape),
                  _const_spec(wd.shape)],
        out_specs=tile,
        compiler_params=_params("parallel"),
        name="ffn",
    )(x, g, wg, wu, wd)


def _out_ffn_kernel(x_ref, a_ref, b_ref, c_ref, wo_ref, g_ref, wg_ref, wu_ref, wd_ref, gf_ref,
                    o_ref, *, final):
    mix = jnp.concatenate([a_ref[...], b_ref[...], c_ref[...]], axis=-1)
    x = x_ref[...] + _dot(mix, wo_ref[...])
    x = _swiglu_residual(x, g_ref[...], wg_ref, wu_ref, wd_ref)
    o_ref[...] = _rms(x, gf_ref[...]) if final else x


def _out_ffn(x, a, b, c, wo, g, wg, wu, wd, gf, final):
    n, d = x.shape
    rows = min(FFN_ROWS, n)

    def tile(w):
        return pl.BlockSpec((rows, w), lambda i: (i, 0))

    return pl.pallas_call(
        functools.partial(_out_ffn_kernel, final=final),
        out_shape=jax.ShapeDtypeStruct((n, d), F32),
        grid=(n // rows,),
        in_specs=[tile(d), tile(a.shape[1]), tile(b.shape[1]), tile(c.shape[1]),
                  _const_spec(wo.shape), _const_spec((1, d)), _const_spec(wg.shape),
                  _const_spec(wu.shape), _const_spec(wd.shape), _const_spec((1, d))],
        out_specs=tile(d),
        compiler_params=_params("parallel"),
        name="out_ffn",
    )(x, a, b, c, wo, g, wg, wu, wd, gf)


def _norm_matmul_kernel(x_ref, g_ref, w_ref, o_ref):
    o_ref[...] = _dot(_rms(x_ref[...], g_ref[...]).astype(BF16), w_ref[...])


def _norm_matmul(x, g, w):
    n, d = x.shape
    return pl.pallas_call(
        _norm_matmul_kernel,
        out_shape=jax.ShapeDtypeStruct((n, w.shape[1]), F32),
        grid=(1,),
        in_specs=[_const_spec((n, d)), _const_spec((1, d)), _const_spec(w.shape)],
        out_specs=pl.BlockSpec((n, w.shape[1]), lambda i: (0, 0)),
        compiler_params=_params("arbitrary"),
        name="norm_matmul",
    )(x, g, w)


def _proj_kernel(x_ref, g_ref, wc_ref, wh_ref, wq_ref, wkt_ref, wvt_ref, cw_ref,
                 a_ref, hh_ref, q_ref, kt_ref, vt_ref, ktb_ref, vtb_ref, cs_ref, carry_ref):
    @pl.when(pl.program_id(1) == 0)
    def _():
        carry_ref[...] = jnp.zeros_like(carry_ref)

    n = _rms(x_ref[...], g_ref[...]).astype(BF16)
    rows = n.shape[0]

    hc = _dot(n, wc_ref[...])
    bg, cg, xa = hc[:, :D_CONV], hc[:, D_CONV:2 * D_CONV], hc[:, 2 * D_CONV:]
    u = cg * xa
    prev = carry_ref[...]
    p1, p2 = prev[SUBLANES - 1:SUBLANES, :], prev[SUBLANES - 2:SUBLANES - 1, :]
    row = lax.broadcasted_iota(jnp.int32, u.shape, 0)
    u1 = jnp.where(row >= 1, pltpu.roll(u, 1, 0), p1)
    u2 = jnp.where(row >= 2, pltpu.roll(u, 2, 0), jnp.where(row == 1, p1, p2))
    cw = cw_ref[...]
    y = cw[0:1, :] * u2 + cw[1:2, :] * u1 + cw[2:3, :] * u
    a_ref[...] = (bg * y).astype(BF16)
    carry_ref[...] = u[rows - SUBLANES:, :]
    cs_ref[...] = u[rows - 2:, :]

    hh_ref[...] = _dot(n, wh_ref[...])
    q_ref[...] = (_dot(n, wq_ref[...]) * (SB_SCALE * LOG2E)).astype(BF16)
    kt = _dot_nt(wkt_ref[...], n).reshape(SB_HEADS, SB_HEAD_DIM, rows)
    vt = _dot_nt(wvt_ref[...], n).reshape(SB_HEADS, SB_HEAD_DIM, rows)
    kt_ref[...] = kt
    vt_ref[...] = vt
    ktb_ref[...] = kt.astype(BF16)
    vtb_ref[...] = vt.astype(BF16)


def _proj(x, g, wc, wh, wq, wkt, wvt, cw):
    b, t, d = x.shape
    rows = FFN_ROWS

    def tile(w):
        return pl.BlockSpec((None, rows, w), lambda i, j: (i, j, 0))

    kv_tile = pl.BlockSpec((None, SB_HEADS, SB_HEAD_DIM, rows), lambda i, j: (i, 0, 0, j))
    kv_shape = (b, SB_HEADS, SB_HEAD_DIM, t)
    return pl.pallas_call(
        _proj_kernel,
        out_shape=(jax.ShapeDtypeStruct((b, t, D_CONV), BF16),
                   jax.ShapeDtypeStruct((b, t, 4 * D_HGRN), F32),
                   jax.ShapeDtypeStruct((b, t, D_SB), BF16),
                   jax.ShapeDtypeStruct(kv_shape, F32), jax.ShapeDtypeStruct(kv_shape, F32),
                   jax.ShapeDtypeStruct(kv_shape, BF16), jax.ShapeDtypeStruct(kv_shape, BF16),
                   jax.ShapeDtypeStruct((b, 2, D_CONV), F32)),
        grid=(b, t // rows),
        in_specs=[tile(d), _const_spec((1, d)), _const_spec(wc.shape), _const_spec(wh.shape),
                  _const_spec(wq.shape), _const_spec(wkt.shape), _const_spec(wvt.shape),
                  _const_spec(cw.shape)],
        out_specs=(tile(D_CONV), tile(4 * D_HGRN), tile(D_SB), kv_tile, kv_tile, kv_tile, kv_tile,
                   pl.BlockSpec((None, 2, D_CONV), lambda i, j: (i, 0, 0))),
        scratch_shapes=[pltpu.VMEM((SUBLANES, D_CONV), F32)],
        compiler_params=_params("arbitrary", "arbitrary"),
        name="proj",
    )(x, g, wc, wh, wq, wkt, wvt, cw)


def _sample_conv_kernel(bg_ref, cg_ref, xa_ref, prev_ref, cw_ref, a_ref, cs_ref):
    steps = bg_ref.shape[0]
    up = [prev_ref[0], prev_ref[1]] + [cg_ref[t] * xa_ref[t] for t in range(steps)]
    cw = cw_ref[...]
    for t in range(steps):
        y = cw[0:1, :] * up[t] + cw[1:2, :] * up[t + 1] + cw[2:3, :] * up[t + 2]
        a_ref[t] = (bg_ref[t] * y).astype(BF16)
    cs_ref[0] = up[steps]
    cs_ref[1] = up[steps + 1]


def _sample_conv(bg, cg, xa, prev, cw):
    full = lambda s: pl.BlockSpec(s, lambda i: (0,) * len(s))
    return pl.pallas_call(
        _sample_conv_kernel,
        out_shape=(jax.ShapeDtypeStruct(bg.shape, BF16), jax.ShapeDtypeStruct(prev.shape, F32)),
        grid=(1,),
        in_specs=[full(bg.shape), full(cg.shape), full(xa.shape), full(prev.shape), full(cw.shape)],
        out_specs=(full(bg.shape), full(prev.shape)),
        compiler_params=_params("arbitrary"),
        name="sample_conv",
    )(bg, cg, xa, prev, cw)


def _hgrn_kernel(hh_ref, lb_ref, gn_ref, s0_ref, tril_ref, ones_ref, o_ref, sfin_ref,
                 st_ref, b_ref, d_ref, ae_ref, oi_ref, *, chunk, valid):
    group = hh_ref.shape[0]
    head_slices = [slice(h * HGRN_HEAD, (h + 1) * HGRN_HEAD) for h in range(D_HGRN // HGRN_HEAD)]

    @pl.when(pl.program_id(1) == 0)
    def _():
        st_ref[...] = jnp.zeros_like(st_ref)
        for s in range(group):
            for h, sl in enumerate(head_slices):
                st_ref[s, sl, sl] = s0_ref[s, h]

    for s in range(group):
        _hgrn_chunk(hh_ref.at[s], lb_ref, gn_ref, tril_ref, ones_ref, o_ref.at[s], st_ref.at[s],
                    b_ref.at[s], d_ref.at[s], ae_ref.at[s], oi_ref.at[s], chunk=chunk, valid=valid)

    @pl.when(pl.program_id(1) == pl.num_programs(1) - 1)
    def _():
        for s in range(group):
            for h, sl in enumerate(head_slices):
                sfin_ref[s, h] = st_ref[s, sl, sl]


def _hgrn_chunk(hh_ref, lb_ref, gn_ref, tril_ref, ones_ref, o_ref, st_ref, b_ref, d_ref, ae_ref,
                oi_ref, *, chunk, valid):
    q = hh_ref[:, 0:D_HGRN]
    z = hh_ref[:, D_HGRN:2 * D_HGRN]
    v = hh_ref[:, 2 * D_HGRN:3 * D_HGRN]
    g = hh_ref[:, 3 * D_HGRN:4 * D_HGRN]
    lb = lb_ref[...]

    _, log_sig = _log_not_sigmoid(z)
    la = jnp.log(lb)
    lc = jnp.log1p(-lb) + log_sig
    logf = jnp.maximum(la, lc) + jnp.log1p(jnp.exp(-jnp.abs(la - lc)))
    k = (1.0 - lb) * jax.nn.sigmoid(-z)
    srow = lax.broadcasted_iota(jnp.int32, (chunk, 1), 0)
    if valid < chunk:
        logf = jnp.where(srow < valid, logf, 0.0)
        k = jnp.where(srow < valid, k, 0.0)

    tril = tril_ref[...]
    b = sum(_dot(tril, part) for part in _split3(logf))
    b_last = b[chunk - 1:chunk, :]
    b_ref[...] = b

    st = st_ref[...]
    o_inter = _dot_nt((q * jnp.exp(b)).astype(BF16), st.astype(BF16))

    sub = min(HGRN_SUB, chunk)
    heads = D_HGRN // HGRN_HEAD
    for t in range(chunk):
        lo = (t // sub) * sub
        bt = b_ref[pl.ds(t, 1), :]
        qt = hh_ref[pl.ds(t, 1), 0:D_HGRN]
        dec = jnp.exp(jnp.where(srow[lo:lo + sub] <= t, bt - b[lo:lo + sub], -jnp.inf))
        d_ref[pl.ds(t * sub, sub), :] = (qt * k[lo:lo + sub]) * dec
    ae_ref[...] = _dot(d_ref[...].astype(BF16), ones_ref[...])
    for t in range(chunk):
        lo = (t // sub) * sub
        oi_ref[pl.ds(t, 1), :] = jnp.sum(ae_ref[pl.ds(t * sub, sub), :] * v[lo:lo + sub], axis=0,
                                         keepdims=True)
    o = o_inter + oi_ref[...]

    hm = [ones_ref[h * HGRN_HEAD:h * HGRN_HEAD + 1, :].astype(F32) for h in range(heads)]
    off = [jnp.zeros((sub, D_HGRN), F32)]
    for i in range(1, chunk // sub):
        lo = i * sub
        ref_b = b[lo - 1:lo]
        qt_i = q[lo:lo + sub] * jnp.exp(b[lo:lo + sub] - ref_b)
        kt_i = k[:lo] * jnp.exp(ref_b - b[:lo])
        qbd = jnp.concatenate([qt_i * hm[h] for h in range(heads)], axis=0).astype(BF16)
        att = _dot_nt(qbd, kt_i.astype(BF16))
        pv = _dot(att.astype(BF16), v[:lo].astype(BF16))
        off.append(sum(pv[h * sub:(h + 1) * sub] * hm[h] for h in range(heads)))
    o = o + jnp.concatenate(off, axis=0)

    head_mask = ones_ref[...].astype(F32)
    kd = k * jnp.exp(b_last - b)
    st_ref[...] = st * jnp.exp(b_last) + _dot_tn(v.astype(BF16), kd.astype(BF16)) * head_mask

    ms = sum(_dot(part, ones_ref[...]) for part in _split2(o * o)) * (1.0 / HGRN_HEAD)
    y = o * lax.rsqrt(ms + RMS_EPS) * gn_ref[...]
    o_ref[...] = (y * (g * jax.nn.sigmoid(g))).astype(BF16)


def _hgrn(hh, lb, gn, s0t, chunk, valid):
    bsz, t, _ = hh.shape
    heads = D_HGRN // HGRN_HEAD
    r = jnp.arange(chunk)
    tril = (r[:, None] >= r[None, :]).astype(BF16)
    hd = jnp.arange(D_HGRN) // HGRN_HEAD
    ones = (hd[:, None] == hd[None, :]).astype(BF16)
    group = HGRN_GROUP
    st_spec = pl.BlockSpec((group, heads, HGRN_HEAD, HGRN_HEAD), lambda i, j: (i, 0, 0, 0))
    pairs = chunk * min(HGRN_SUB, chunk)
    return pl.pallas_call(
        functools.partial(_hgrn_kernel, chunk=chunk, valid=valid),
        out_shape=(jax.ShapeDtypeStruct((bsz, t, D_HGRN), BF16),
                   jax.ShapeDtypeStruct(s0t.shape, F32)),
        grid=(bsz // group, t // chunk),
        in_specs=[pl.BlockSpec((group, chunk, 4 * D_HGRN), lambda i, j: (i, j, 0)),
                  _const_spec((1, D_HGRN)), _const_spec((1, D_HGRN)), st_spec,
                  _const_spec(tril.shape), _const_spec(ones.shape)],
        out_specs=(pl.BlockSpec((group, chunk, D_HGRN), lambda i, j: (i, j, 0)), st_spec),
        scratch_shapes=[pltpu.VMEM((group, D_HGRN, D_HGRN), F32),
                        pltpu.VMEM((group, chunk, D_HGRN), F32),
                        pltpu.VMEM((group, pairs, D_HGRN), F32),
                        pltpu.VMEM((group, pairs, D_HGRN), F32),
                        pltpu.VMEM((group, chunk, D_HGRN), F32)],
        compiler_params=_params("arbitrary", "arbitrary"),
        name="hgrn",
    )(hh, lb, gn, s0t, tril, ones)


def _suffix_ones(n):
    r = jnp.arange(n)
    return (r[:, None] > r[None, :]).astype(BF16)


def _suffix_and_total(n):
    return jnp.concatenate([_suffix_ones(n), jnp.ones((n, n), BF16)], axis=1)


def _neg_log2_not_sigmoid(z):
    pos = jnp.maximum(z, 0.0)
    neg = z - pos
    soft = jnp.log(1.0 + jnp.exp2(neg - pos)) * LOG2E
    return pos + soft, neg - soft


def _attn_kernel(bias_ref, q_ref, kt_ref, vt_ref, u_ref, o_ref, acc_ref, car_ref, ls_ref, r_ref,
                 *, tq, tk):
    i = pl.program_id(1)
    heads = q_ref.shape[1] // SB_HEAD_DIM
    bias = [bias_ref[h] * LOG2E for h in range(heads)]
    lane_tiles = tk // LANES

    def col(j):
        return pl.ds(pl.multiple_of(j * tk, tk), tk)

    def scores(h, j):
        return _dot(q_ref[:, h * SB_HEAD_DIM:(h + 1) * SB_HEAD_DIM], kt_ref[h, :, col(j)])

    def logs(h, z, keep):
        p, log_sig = _neg_log2_not_sigmoid(z + bias[h])
        if keep is not None:
            p = jnp.where(keep, p, 0.0)
            log_sig = jnp.where(keep, log_sig, -jnp.inf)
        ls_ref[h] = log_sig
        r_ref[h] = _dot(p.astype(BF16), u_ref[...])
        return jnp.broadcast_to(jnp.sum(p, axis=1, keepdims=True), (tq, LANES))

    def weights(h, j):
        after = r_ref[h] + jnp.concatenate([car_ref[h]] * lane_tiles, axis=1)
        w = jnp.exp2(ls_ref[h] - after)
        acc_ref[h] += _dot_nt(w.astype(BF16), vt_ref[h, :, col(j)])

    keep = (lax.broadcasted_iota(jnp.int32, (tq, tk), 1) < lax.broadcasted_iota(jnp.int32, (tq, tk), 0))
    for h in range(heads):
        car_ref[h] = jnp.zeros((tq, LANES), F32)
        car_ref[heads + h] = logs(h, scores(h, i), keep)
        acc_ref[h] = jnp.zeros((tq, SB_HEAD_DIM), F32)

    def body(jj, carry):
        j = i - 1 - jj
        zs = [scores(h, j) for h in range(heads)]
        for h in range(heads):
            weights(h, j + 1)
        for h in range(heads):
            total = logs(h, zs[h], None)
            c = car_ref[heads + h]
            car_ref[h] = c
            car_ref[heads + h] = c + total
        return carry

    lax.fori_loop(0, i, body, 0)
    for h in range(heads):
        weights(h, 0)
    o_ref[...] = jnp.concatenate([acc_ref[h] for h in range(heads)], axis=-1).astype(BF16)


def _attn(q, ktb, vtb, bias):
    b, t, _ = q.shape
    tq = tk = min(ATT_Q, t)
    kv_spec = pl.BlockSpec((None, SB_HEADS, SB_HEAD_DIM, t), lambda bi, i: (bi, 0, 0, 0),
                           pipeline_mode=pl.Buffered(1))
    qo_spec = pl.BlockSpec((None, tq, D_SB), lambda bi, i: (bi, i, 0))
    return pl.pallas_call(
        functools.partial(_attn_kernel, tq=tq, tk=tk),
        out_shape=jax.ShapeDtypeStruct(q.shape, BF16),
        grid=(b, t // tq),
        in_specs=[_smem_spec(), qo_spec, kv_spec, kv_spec, _const_spec((tk, tk))],
        out_specs=qo_spec,
        scratch_shapes=[pltpu.VMEM((SB_HEADS, tq, SB_HEAD_DIM), F32),
                        pltpu.VMEM((2 * SB_HEADS, tq, LANES), F32),
                        pltpu.VMEM((SB_HEADS, tq, tk), F32),
                        pltpu.VMEM((SB_HEADS, tq, tk), F32)],
        compiler_params=_params("parallel", "arbitrary"),
        name="attn",
    )(bias, q, ktb, vtb, _suffix_ones(tk))


def _paged_attn_kernel(pt_ref, bias_ref, q_ref, knew_ref, vnew_ref, ue_ref, ck_ref, cv_ref, o_ref,
                       kbuf, vbuf, sem, acc_ref, car_ref, *, layer, pages, group):
    s = pl.program_id(0)
    nseq = pl.num_programs(0)
    chunks = pages // group
    rows = SB_HEADS * SUBLANES

    def copies(seq, c, slot):
        out = []
        for g in range(group):
            page = pt_ref[seq * pages + (chunks - 1 - c) * group + g]
            lanes = pl.ds(g * LANES, LANES)
            out.append(pltpu.make_async_copy(ck_ref.at[layer, page], kbuf.at[slot, :, :, lanes],
                                             sem.at[0, slot]))
            out.append(pltpu.make_async_copy(cv_ref.at[layer, page], vbuf.at[slot, :, :, lanes],
                                             sem.at[1, slot]))
        return out

    def start(seq, c, slot):
        for cp in copies(seq, c, slot):
            cp.start()

    def wait(seq, c, slot):
        for cp in copies(seq, c, slot):
            cp.wait()

    @pl.when(s == 0)
    def _():
        start(0, 0, 0)

    acc_ref[...] = jnp.zeros_like(acc_ref)
    car_ref[...] = jnp.zeros_like(car_ref)
    q = (q_ref[...] * (SB_SCALE * LOG2E)).astype(BF16)
    bias = [bias_ref[h] * LOG2E for h in range(SB_HEADS)]

    def block(kt_of, vt_of, npages, keep):
        z = jnp.concatenate(
            [_dot(q[h], kt_of(h).astype(BF16)) + bias[h] for h in range(SB_HEADS)], axis=0)
        p, log_sig = _neg_log2_not_sigmoid(z)
        if keep is not None:
            p = jnp.where(keep, p, 0.0)
            log_sig = jnp.where(keep, log_sig, -jnp.inf)
        stacked = jnp.concatenate([p[:, g * LANES:(g + 1) * LANES] for g in range(npages)], axis=0)
        sums = _dot(stacked.astype(BF16), ue_ref[...])
        car = car_ref[...]
        after = [None] * npages
        for g in range(npages - 1, -1, -1):
            page_sums = sums[g * rows:(g + 1) * rows]
            after[g] = page_sums[:, :LANES] + car
            car = car + page_sums[:, LANES:]
        car_ref[...] = car
        w = jnp.exp2(log_sig - jnp.concatenate(after, axis=1)).astype(BF16)
        for h in range(SB_HEADS):
            acc_ref[h] += _dot_nt(w[h * SUBLANES:(h + 1) * SUBLANES], vt_of(h).astype(BF16))

    step = lax.broadcasted_iota(jnp.int32, (rows, LANES), 0) % SUBLANES
    keep_new = lax.broadcasted_iota(jnp.int32, (rows, LANES), 1) < step
    block(lambda h: knew_ref[h], lambda h: vnew_ref[h], 1, keep_new)

    def chunk_body(c, carry):
        slot = c % 2

        @pl.when(c + 1 < chunks)
        def _():
            start(s, c + 1, 1 - slot)

        @pl.when(jnp.logical_and(c + 1 == chunks, s + 1 < nseq))
        def _():
            start(s + 1, 0, 1 - slot)

        wait(s, c, slot)
        block(lambda h: kbuf[slot, h], lambda h: vbuf[slot, h], group, None)
        return carry

    lax.fori_loop(0, chunks, chunk_body, 0)
    o_ref[...] = acc_ref[...]


def _paged_attn(page_table, bias, q, knew, vnew, ck, cv, layer):
    nseq, pages = page_table.shape
    page = ck.shape[-1]
    group = DEC_PAGES
    while pages % (2 * group):
        group //= 2
    row = lambda shp: pl.BlockSpec((None,) + shp, lambda i, *_: (i,) + (0,) * len(shp))
    grid_spec = pltpu.PrefetchScalarGridSpec(
        num_scalar_prefetch=1,
        grid=(nseq,),
        in_specs=[_smem_spec(), row(q.shape[1:]), row(knew.shape[1:]), row(vnew.shape[1:]),
                  pl.BlockSpec((page, 2 * page), lambda i, *_: (0, 0)),
                  pl.BlockSpec(memory_space=pl.ANY), pl.BlockSpec(memory_space=pl.ANY)],
        out_specs=row(q.shape[1:]),
        scratch_shapes=[pltpu.VMEM((2, SB_HEADS, SB_HEAD_DIM, group * page), F32),
                        pltpu.VMEM((2, SB_HEADS, SB_HEAD_DIM, group * page), F32),
                        pltpu.SemaphoreType.DMA((2, 2)),
                        pltpu.VMEM((SB_HEADS, SUBLANES, SB_HEAD_DIM), F32),
                        pltpu.VMEM((SB_HEADS * SUBLANES, LANES), F32)],
    )
    return pl.pallas_call(
        functools.partial(_paged_attn_kernel, layer=layer, pages=pages, group=group),
        out_shape=jax.ShapeDtypeStruct(q.shape, F32),
        grid_spec=grid_spec,
        compiler_params=_params("arbitrary"),
        name="paged_attn",
    )(page_table.reshape(-1), bias, q, knew, vnew, _suffix_and_total(page), ck, cv)


def _lower_bounds(lb_logits):
    p = jax.nn.softmax(lb_logits.astype(F32), axis=0)
    c = jnp.cumsum(p, axis=0)
    return c - c[0:1]


def kernel(x_prompt, x_sample, state_conv, state_hgrn, cache_k, cache_v, page_table, norm_ffn1, w_ffn1_gate, w_ffn1_up, w_ffn1_down, norm_mix, w_in, conv_w, hgrn_lb_logits, hgrn_gnorm, sb_bias, w_out, norm_ffn2, w_ffn2_gate, w_ffn2_up, w_ffn2_down, norm_final):
    depth = w_in.shape[0]
    bp, tp, d = x_prompt.shape
    bs, ts, _ = x_sample.shape
    heads_h = D_HGRN // HGRN_HEAD
    lower = _lower_bounds(hgrn_lb_logits)
    ck = jnp.transpose(cache_k, (0, 1, 3, 4, 2))
    cv = jnp.transpose(cache_v, (0, 1, 3, 4, 2))
    page = ck.shape[-1]
    bf = lambda w: w.astype(BF16)
    row = lambda v: v.reshape(1, -1)
    o_conv, o_hgrn, o_q = 0, 3 * D_CONV, 3 * D_CONV + 4 * D_HGRN
    o_k, o_v = o_q + D_SB, o_q + 2 * D_SB

    xp = x_prompt.reshape(bp * tp, d)
    xs = x_sample.reshape(bs * ts, d)
    conv_p, conv_s, hg_p, hg_s, kp_l, vp_l, ks_l, vs_l = [], [], [], [], [], [], [], []
    for l in range(depth):
        final = l == depth - 1
        w1 = (row(norm_ffn1[l]), bf(w_ffn1_gate[l]), bf(w_ffn1_up[l]), bf(w_ffn1_down[l]))
        w2 = (row(norm_ffn2[l]), bf(w_ffn2_gate[l]), bf(w_ffn2_up[l]), bf(w_ffn2_down[l]))
        wi = bf(w_in[l])
        wo = bf(w_out[l])
        gmix, gfin = row(norm_mix[l]), row(norm_final)
        lb, gn = row(lower[l]), row(hgrn_gnorm[l])

        xp = _ffn(xp, *w1)
        a_out, hh, q, kt, vt, ktb, vtb, cst = _proj(
            xp.reshape(bp, tp, d), gmix, wi[:, o_conv:o_hgrn], wi[:, o_hgrn:o_q], wi[:, o_q:o_k],
            wi[:, o_k:o_v].T, wi[:, o_v:].T, conv_w[l])
        b_out, sfin = _hgrn(hh, lb, gn, jnp.zeros((bp, heads_h, HGRN_HEAD, HGRN_HEAD), F32),
                            HGRN_CHUNK, HGRN_CHUNK)
        c_out = _attn(q, ktb, vtb, sb_bias[l])
        xp = _out_ffn(xp, a_out.reshape(bp * tp, -1), b_out.reshape(bp * tp, -1),
                      c_out.reshape(bp * tp, -1), wo, *w2, gfin, final)
        conv_p.append(cst)
        hg_p.append(jnp.swapaxes(sfin, -1, -2))
        kp_l.append(jnp.transpose(kt, (0, 3, 1, 2)))
        vp_l.append(jnp.transpose(vt, (0, 3, 1, 2)))

        xs = _ffn(xs, *w1)
        hs = _norm_matmul(xs, gmix, wi)
        tm = lambda c0, w: jnp.swapaxes(hs[:, c0:c0 + w].reshape(bs, ts, w), 0, 1)
        a_s, cs_new = _sample_conv(tm(0, D_CONV), tm(D_CONV, D_CONV), tm(2 * D_CONV, D_CONV),
                                   jnp.swapaxes(state_conv[l], 0, 1), conv_w[l])
        hh_s = jnp.pad(hs[:, o_hgrn:o_q].reshape(bs, ts, -1), ((0, 0), (0, SUBLANES - ts), (0, 0)))
        b_s, sfin_s = _hgrn(hh_s, lb, gn, jnp.swapaxes(state_hgrn[l], -1, -2), SUBLANES, ts)
        heads4 = lambda c0: hs[:, c0:c0 + D_SB].reshape(bs, ts, SB_HEADS, SB_HEAD_DIM)
        q_s = jnp.pad(jnp.swapaxes(heads4(o_q), 1, 2), ((0, 0), (0, 0), (0, SUBLANES - ts), (0, 0)))
        k_s, v_s = heads4(o_k), heads4(o_v)
        new_t = lambda a: jnp.pad(jnp.transpose(a, (0, 2, 3, 1)),
                                  ((0, 0), (0, 0), (0, 0), (0, page - ts)))
        c_s = _paged_attn(page_table, sb_bias[l], q_s, new_t(k_s), new_t(v_s), ck, cv, l)
        c_s = jnp.swapaxes(c_s[:, :, :ts, :], 1, 2).reshape(bs * ts, D_SB).astype(BF16)
        xs = _out_ffn(xs, jnp.swapaxes(a_s, 0, 1).reshape(bs * ts, -1),
                      b_s[:, :ts, :].reshape(bs * ts, -1), c_s, wo, *w2, gfin, final)
        conv_s.append(jnp.swapaxes(cs_new, 0, 1))
        hg_s.append(jnp.swapaxes(sfin_s, -1, -2))
        ks_l.append(k_s)
        vs_l.append(v_s)

    return (xp.reshape(bp, tp, d), xs.reshape(bs, ts, d), jnp.stack(conv_p), jnp.stack(conv_s),
            jnp.stack(hg_p), jnp.stack(hg_s), jnp.stack(kp_l), jnp.stack(vp_l),
            jnp.stack(ks_l), jnp.stack(vs_l))
```
